```python
import math
import jax
import jax.numpy as jnp
from jax import lax
import numpy as np


D_MODEL = 1024
BATCH = 8
SEQ = 4096
DEPTH = 4

D_MIX = D_MODEL
D_REC = D_MIX // 2
D_SSM = D_MIX - D_REC
REC_HEADS = 8
REC_HEAD_DIM = D_REC // REC_HEADS
CONV_WIDTH = 4
LRU_C = 8.0
SSM_GROUP = 16
SSM_GROUPS = D_SSM // SSM_GROUP
SSM_STATE = 64
N_EXPERT_GROUPS = 4
EXPERTS_PER_GROUP = 8
N_EXPERTS = N_EXPERT_GROUPS * EXPERTS_PER_GROUP
TOP_K = 2
D_EXPERT = D_MODEL // 2
MOE_BLOCK = 128
ALPHA = (2.0 * DEPTH) ** 0.25
BETA = (8.0 * DEPTH) ** -0.25
LN_EPS = 1e-5
RMS_EPS = 1e-6

kernel_name = 'hybrid_rglru_s5_hmoe'


def layer_norm(x, g, b):
    xf = x.astype(jnp.float32)
    mu = jnp.mean(xf, axis=-1, keepdims=True)
    var = jnp.mean(jnp.square(xf - mu), axis=-1, keepdims=True)
    return ((xf - mu) * lax.rsqrt(var + LN_EPS) * g.astype(jnp.float32) + b.astype(jnp.float32)).astype(x.dtype)


def rms_norm(x, g):
    xf = x.astype(jnp.float32)
    return xf * lax.rsqrt(jnp.mean(jnp.square(xf), axis=-1, keepdims=True) + RMS_EPS) * g.astype(jnp.float32)


def causal_depthwise_conv(u, w, b):
    L = u.shape[1]
    up = jnp.pad(u, ((0, 0), (CONV_WIDTH - 1, 0), (0, 0)))
    out = b
    for k in range(CONV_WIDTH):
        out = out + w[k] * up[:, k:k + L]
    return out


def _linear_combine(c1, c2):
    a1, b1 = c1
    a2, b2 = c2
    return a1 * a2, a2 * b1 + b2


def _complex_linear_combine(c1, c2):
    a1r, a1i, b1r, b1i = c1
    a2r, a2i, b2r, b2i = c2
    ar = a1r * a2r - a1i * a2i
    ai = a1r * a2i + a1i * a2r
    br = a2r * b1r - a2i * b1i + b2r
    bi = a2r * b1i + a2i * b1r + b2i
    return ar, ai, br, bi


def rg_lru(u, wa, ba, wx, bx, lam):
    B, L, _ = u.shape
    uf = u.astype(jnp.float32)
    uh = uf.reshape(B, L, REC_HEADS, REC_HEAD_DIM)
    r = jax.nn.sigmoid(jnp.einsum('blhi,hij->blhj', uh, wa.astype(jnp.float32)).reshape(B, L, D_REC) + ba.astype(jnp.float32))
    i = jax.nn.sigmoid(jnp.einsum('blhi,hij->blhj', uh, wx.astype(jnp.float32)).reshape(B, L, D_REC) + bx.astype(jnp.float32))
    log_a = -LRU_C * r * jax.nn.softplus(-lam.astype(jnp.float32))
    a = jnp.exp(log_a)
    mult = jnp.sqrt(-jnp.expm1(2.0 * log_a))
    bterm = mult * (i * uf)
    _, h = lax.associative_scan(_linear_combine, (a, bterm), axis=1)
    return h


def s5_ssm(u, lam_re, lam_im, log_dt, b_re, b_im, c_re, c_im, d_skip, w_glu, b_glu):
    B, L, _ = u.shape
    uf = u.astype(jnp.float32).reshape(B, L, SSM_GROUPS, SSM_GROUP)
    lre = jnp.minimum(lam_re.astype(jnp.float32), -1e-4)
    lim = lam_im.astype(jnp.float32)
    dt = jnp.exp(log_dt.astype(jnp.float32))[:, None]
    mag = jnp.exp(lre * dt)
    abar_re = mag * jnp.cos(lim * dt)
    abar_im = mag * jnp.sin(lim * dt)
    den = lre * lre + lim * lim
    p_re = abar_re - 1.0
    p_im = abar_im
    coef_re = (p_re * lre + p_im * lim) / den
    coef_im = (p_im * lre - p_re * lim) / den
    br = b_re.astype(jnp.float32)
    bi = b_im.astype(jnp.float32)
    bbar_re = coef_re[..., None] * br - coef_im[..., None] * bi
    bbar_im = coef_re[..., None] * bi + coef_im[..., None] * br
    bu_re = jnp.einsum('blgc,gnc->blgn', uf, bbar_re)
    bu_im = jnp.einsum('blgc,gnc->blgn', uf, bbar_im)
    ar = jnp.broadcast_to(abar_re, bu_re.shape)
    ai = jnp.broadcast_to(abar_im, bu_im.shape)
    _, _, xr, xi = lax.associative_scan(_complex_linear_combine, (ar, ai, bu_re, bu_im), axis=1)
    y = (jnp.einsum('blgn,gcn->blgc', xr, c_re.astype(jnp.float32))
         - jnp.einsum('blgn,gcn->blgc', xi, c_im.astype(jnp.float32))
         + d_skip.astype(jnp.float32) * uf)
    y = jax.nn.gelu(y.reshape(B, L, D_SSM))
    z = y @ w_glu.astype(jnp.float32) + b_glu.astype(jnp.float32)
    return z[..., :D_SSM] * jax.nn.sigmoid(z[..., D_SSM:])


def hybrid_mixer(x, w_in, conv_w, conv_b, lru_wa, lru_ba, lru_wx, lru_bx, lru_lambda,
                 ssm_lambda_re, ssm_lambda_im, ssm_log_dt, ssm_b_re, ssm_b_im, ssm_c_re, ssm_c_im,
                 ssm_d, w_glu, b_glu, g_rec, g_ssm, w_out):
    proj = x @ w_in
    gate_br = proj[..., :D_REC]
    rec_br = proj[..., D_REC:2 * D_REC]
    ssm_br = proj[..., 2 * D_REC:]
    rec = causal_depthwise_conv(rec_br, conv_w, conv_b)
    h = rg_lru(rec, lru_wa, lru_ba, lru_wx, lru_bx, lru_lambda)
    y_rec = jax.nn.gelu(gate_br.astype(jnp.float32)) * h
    y_ssm = s5_ssm(ssm_br, ssm_lambda_re, ssm_lambda_im, ssm_log_dt, ssm_b_re, ssm_b_im,
                   ssm_c_re, ssm_c_im, ssm_d, w_glu, b_glu)
    y = jnp.concatenate([rms_norm(y_rec, g_rec), rms_norm(y_ssm, g_ssm)], axis=-1).astype(x.dtype)
    return y @ w_out


def hierarchical_moe(x, router_wg, router_bg, router_we, router_be, exp_w_gate, exp_w_up, exp_w_down):
    B, L, D = x.shape
    T = B * L
    xt = x.reshape(T, D)
    g_prob = jax.nn.softmax((xt @ router_wg).astype(jnp.float32) + router_bg.astype(jnp.float32), axis=-1)
    g_top, g_idx = lax.top_k(g_prob, 1)
    e_logits = ((xt @ router_we).astype(jnp.float32) + router_be.astype(jnp.float32)).reshape(T, N_EXPERT_GROUPS, EXPERTS_PER_GROUP)
    e_in = jnp.take_along_axis(e_logits, g_idx[:, :, None], axis=1)[:, 0]
    e_prob = jax.nn.softmax(e_in, axis=-1)
    e_top, e_idx = lax.top_k(e_prob, TOP_K)
    gates = g_top * e_top / jnp.sum(e_top, axis=-1, keepdims=True)
    expert_id = (g_idx * EXPERTS_PER_GROUP + e_idx).reshape(-1)
    token_id = jnp.repeat(jnp.arange(T, dtype=jnp.int32), TOP_K)
    gate_flat = gates.reshape(-1)
    order = jnp.argsort(expert_id)
    sorted_e = expert_id[order]
    counts = jnp.bincount(expert_id, length=N_EXPERTS)
    starts = jnp.cumsum(counts) - counts
    padded = (counts + MOE_BLOCK - 1) // MOE_BLOCK * MOE_BLOCK
    pad_ends = jnp.cumsum(padded)
    pad_starts = pad_ends - padded
    j = jnp.arange(T * TOP_K, dtype=jnp.int32)
    dest = pad_starts[sorted_e] + j - starts[sorted_e]
    n_blocks = -(-(T * TOP_K) // MOE_BLOCK) + N_EXPERTS
    P = n_blocks * MOE_BLOCK
    buf_tok = jnp.full((P,), T, dtype=jnp.int32).at[dest].set(token_id[order])
    buf_gate = jnp.zeros((P,), jnp.float32).at[dest].set(gate_flat[order])
    block_e = jnp.minimum(jnp.searchsorted(pad_ends, jnp.arange(n_blocks, dtype=jnp.int32) * MOE_BLOCK, side='right'), N_EXPERTS - 1)
    x_pad = jnp.concatenate([xt, jnp.zeros((1, D), xt.dtype)], axis=0)
    xb = x_pad[buf_tok].reshape(n_blocks, MOE_BLOCK, D)

    def expert_block(args):
        xblk, e = args
        hdn = jax.nn.silu(xblk @ exp_w_gate[e]) * (xblk @ exp_w_up[e])
        return hdn @ exp_w_down[e]

    yb = lax.map(expert_block, (xb, block_e)).reshape(P, D)
    y = jax.ops.segment_sum(yb * buf_gate[:, None].astype(yb.dtype), buf_tok, num_segments=T + 1)[:T]
    return y.reshape(B, L, D)


def setup_inputs(seed: int = 0) -> dict:
    key = jax.random.key(seed)
    ks = jax.random.split(key, 40)
    Lc = DEPTH
    f32 = jnp.float32

    def nrm(k, shape, scale):
        return jax.random.normal(k, shape, f32) * scale

    x = nrm(ks[0], (BATCH, SEQ, D_MODEL), 1.0)
    w_in = nrm(ks[1], (Lc, D_MODEL, 2 * D_REC + D_SSM), D_MODEL ** -0.5)
    conv_w = nrm(ks[2], (Lc, CONV_WIDTH, D_REC), CONV_WIDTH ** -0.5)
    conv_b = nrm(ks[3], (Lc, D_REC), 0.01)
    lru_wa = nrm(ks[4], (Lc, REC_HEADS, REC_HEAD_DIM, REC_HEAD_DIM), REC_HEAD_DIM ** -0.5)
    lru_ba = nrm(ks[5], (Lc, D_REC), 0.01)
    lru_wx = nrm(ks[6], (Lc, REC_HEADS, REC_HEAD_DIM, REC_HEAD_DIM), REC_HEAD_DIM ** -0.5)
    lru_bx = nrm(ks[7], (Lc, D_REC), 0.01)
    a0 = jax.random.uniform(ks[8], (Lc, D_REC), f32, minval=0.9, maxval=0.999)
    lru_lambda = jnp.log(a0) - jnp.log1p(-a0)
    n_idx = jnp.arange(SSM_STATE, dtype=f32)
    ssm_lambda_re = -0.5 + nrm(ks[9], (Lc, SSM_GROUPS, SSM_STATE), 0.01)
    ssm_lambda_im = jnp.pi * n_idx + nrm(ks[10], (Lc, SSM_GROUPS, SSM_STATE), 0.01)
    ssm_log_dt = jax.random.uniform(ks[11], (Lc, SSM_GROUPS), f32, minval=math.log(1e-3), maxval=math.log(1e-1))
    ssm_b_re = nrm(ks[12], (Lc, SSM_GROUPS, SSM_STATE, SSM_GROUP), (2.0 * SSM_GROUP) ** -0.5)
    ssm_b_im = nrm(ks[13], (Lc, SSM_GROUPS, SSM_STATE, SSM_GROUP), (2.0 * SSM_GROUP) ** -0.5)
    ssm_c_re = nrm(ks[14], (Lc, SSM_GROUPS, SSM_GROUP, SSM_STATE), SSM_STATE ** -0.5)
    ssm_c_im = nrm(ks[15], (Lc, SSM_GROUPS, SSM_GROUP, SSM_STATE), SSM_STATE ** -0.5)
    ssm_d = nrm(ks[16], (Lc, SSM_GROUPS, SSM_GROUP), 1.0)
    w_glu = nrm(ks[17], (Lc, D_SSM, 2 * D_SSM), D_SSM ** -0.5)
    b_glu = nrm(ks[18], (Lc, 2 * D_SSM), 0.01)
    g_rec = 1.0 + nrm(ks[19], (Lc, D_REC), 0.01)
    g_ssm = 1.0 + nrm(ks[20], (Lc, D_SSM), 0.01)
    w_out = nrm(ks[21], (Lc, D_MIX, D_MODEL), (D_MIX ** -0.5) * BETA)
    ln1_g = 1.0 + nrm(ks[22], (Lc, D_MODEL), 0.01)
    ln1_b = nrm(ks[23], (Lc, D_MODEL), 0.01)
    router_wg = nrm(ks[24], (Lc, D_MODEL, N_EXPERT_GROUPS), D_MODEL ** -0.5)
    router_bg = nrm(ks[25], (Lc, N_EXPERT_GROUPS), 0.01)
    router_we = nrm(ks[26], (Lc, D_MODEL, N_EXPERTS), D_MODEL ** -0.5)
    router_be = nrm(ks[27], (Lc, N_EXPERTS), 0.01)
    exp_w_gate = nrm(ks[28], (Lc, N_EXPERTS, D_MODEL, D_EXPERT), D_MODEL ** -0.5)
    exp_w_up = nrm(ks[29], (Lc, N_EXPERTS, D_MODEL, D_EXPERT), D_MODEL ** -0.5)
    exp_w_down = nrm(ks[30], (Lc, N_EXPERTS, D_EXPERT, D_MODEL), (D_EXPERT ** -0.5) * BETA)
    ln2_g = 1.0 + nrm(ks[31], (Lc, D_MODEL), 0.01)
    ln2_b = nrm(ks[32], (Lc, D_MODEL), 0.01)
    return {'x': x, 'w_in': w_in, 'conv_w': conv_w, 'conv_b': conv_b,
            'lru_wa': lru_wa, 'lru_ba': lru_ba, 'lru_wx': lru_wx, 'lru_bx': lru_bx, 'lru_lambda': lru_lambda,
            'ssm_lambda_re': ssm_lambda_re, 'ssm_lambda_im': ssm_lambda_im, 'ssm_log_dt': ssm_log_dt,
            'ssm_b_re': ssm_b_re, 'ssm_b_im': ssm_b_im, 'ssm_c_re': ssm_c_re, 'ssm_c_im': ssm_c_im,
            'ssm_d': ssm_d, 'w_glu': w_glu, 'b_glu': b_glu, 'g_rec': g_rec, 'g_ssm': g_ssm,
            'w_out': w_out, 'ln1_g': ln1_g, 'ln1_b': ln1_b,
            'router_wg': router_wg, 'router_bg': router_bg, 'router_we': router_we, 'router_be': router_be,
            'exp_w_gate': exp_w_gate, 'exp_w_up': exp_w_up, 'exp_w_down': exp_w_down,
            'ln2_g': ln2_g, 'ln2_b': ln2_b}


def reference(x, w_in, conv_w, conv_b, lru_wa, lru_ba, lru_wx, lru_bx, lru_lambda,
              ssm_lambda_re, ssm_lambda_im, ssm_log_dt, ssm_b_re, ssm_b_im, ssm_c_re, ssm_c_im,
              ssm_d, w_glu, b_glu, g_rec, g_ssm, w_out, ln1_g, ln1_b,
              router_wg, router_bg, router_we, router_be, exp_w_gate, exp_w_up, exp_w_down,
              ln2_g, ln2_b):
    for l in range(DEPTH):
        mix = hybrid_mixer(x, w_in[l], conv_w[l], conv_b[l], lru_wa[l], lru_ba[l], lru_wx[l], lru_bx[l],
                           lru_lambda[l], ssm_lambda_re[l], ssm_lambda_im[l], ssm_log_dt[l],
                           ssm_b_re[l], ssm_b_im[l], ssm_c_re[l], ssm_c_im[l], ssm_d[l],
                           w_glu[l], b_glu[l], g_rec[l], g_ssm[l], w_out[l])
        x = layer_norm(ALPHA * x + mix, ln1_g[l], ln1_b[l])
        ff = hierarchical_moe(x, router_wg[l], router_bg[l], router_we[l], router_be[l],
                              exp_w_gate[l], exp_w_up[l], exp_w_down[l])
        x = layer_norm(ALPHA * x + ff, ln2_g[l], ln2_b[l])
    return x
```

```python
import functools
import math

import jax
import jax.numpy as jnp
from jax import lax
from jax.experimental import pallas as pl
from jax.experimental.pallas import tpu as pltpu

D_MODEL = 1024
BATCH = 8
SEQ = 4096
DEPTH = 4
D_REC = 512
D_SSM = 512
REC_HEADS = 8
REC_HEAD_DIM = 64
CONV_WIDTH = 4
LRU_C = 8.0
SSM_GROUP = 16
SSM_GROUPS = 32
SSM_STATE = 64
N_EXPERT_GROUPS = 4
EXPERTS_PER_GROUP = 8
N_EXPERTS = 32
D_EXPERT = 512
ALPHA = (2.0 * DEPTH) ** 0.25
LN_EPS = 1e-5
RMS_EPS = 1e-6

T_TOK = BATCH * SEQ

SUBLANES = 8
LANES = 128
TL = 32
R_MIX = TL * BATCH
SCAN_UNROLL = 8
SSM_CHUNKS = 4
CH_W = D_SSM // SSM_CHUNKS
ST_W = SSM_GROUPS * SSM_STATE // SSM_CHUNKS
ROUTER_W = 128
E_LOGIT_OFF = 8
MOE_BLK = 256
N_BLOCKS = (T_TOK * 2) // MOE_BLK + N_EXPERTS
P_ROWS = N_BLOCKS * MOE_BLK
TD = 512
TC = 256
VMEM_LIMIT = 56 * 1024 * 1024

f32 = jnp.float32
bf16 = jnp.bfloat16
i32 = jnp.int32


def _full(shape):
    n = len(shape)
    return pl.BlockSpec(shape, lambda *_: (0,) * n)


def _sigmoid(v):
    return 1.0 / (1.0 + jnp.exp(-v))


def _gelu(v):
    return 0.5 * v * (1.0 + jnp.tanh(0.7978845608028654 * (v + 0.044715 * (v * v * v))))


def _unrolled_loop(n_steps, unroll, body, init):
    def outer(o, carry):
        for u in range(unroll):
            carry = body(o * unroll + u, carry)
        return carry
    return lax.fori_loop(0, n_steps // unroll, outer, init)


def _ssm_prep_kernel(lre_ref, lim_ref, ldt_ref, bre_ref, bim_ref,
                     are_ref, aim_ref, bbre_ref, bbim_ref):
    lre = jnp.minimum(lre_ref[...], -1e-4)
    lim = lim_ref[...]
    dt = jnp.exp(ldt_ref[...])
    mag = jnp.exp(lre * dt)
    a_re = mag * jnp.cos(lim * dt)
    a_im = mag * jnp.sin(lim * dt)
    den = lre * lre + lim * lim
    p_re = a_re - 1.0
    p_im = a_im
    coef_re = (p_re * lre + p_im * lim) / den
    coef_im = (p_im * lre - p_re * lim) / den
    br = bre_ref[...]
    bi = bim_ref[...]
    are_ref[...] = a_re
    aim_ref[...] = a_im
    bbre_ref[...] = coef_re * br - coef_im * bi
    bbim_ref[...] = coef_re * bi + coef_im * br


def _ssm_prep(lam_re, lam_im, log_dt, b_re, b_im):
    rows = DEPTH * SSM_GROUPS * SSM_GROUP

    def rep(a):
        return jnp.broadcast_to(a[:, :, None, :], (DEPTH, SSM_GROUPS, SSM_GROUP, SSM_STATE)).reshape(rows, SSM_STATE)

    ldt = jnp.broadcast_to(log_dt[:, :, None], (DEPTH, SSM_GROUPS, SSM_STATE))
    bre_t = jnp.transpose(b_re, (0, 1, 3, 2)).reshape(rows, SSM_STATE)
    bim_t = jnp.transpose(b_im, (0, 1, 3, 2)).reshape(rows, SSM_STATE)
    shp = jax.ShapeDtypeStruct((rows, SSM_STATE), f32)
    a_re, a_im, bb_re, bb_im = pl.pallas_call(
        _ssm_prep_kernel, out_shape=(shp, shp, shp, shp), name="ssm_prep",
    )(rep(lam_re), rep(lam_im), rep(ldt), bre_t, bim_t)
    return a_re, a_im, bb_re, bb_im


def _mixer_kernel(x_ref, w_in_ref, conv_w_ref, conv_b_ref, w_gate_ref, ba_ref, bx_ref, lam_ref,
                  abar_re_ref, abar_im_ref, bbar_ref, cmat_ref, dskip_ref, w_glu_ref, b_glu_ref,
                  g_rec_ref, g_ssm_ref, w_out_ref, ln_g_ref, ln_b_ref, wr_ref, rb_ref, tri_ref,
                  x1_ref, eid_ref, gate_ref, rank_ref, cnt_ref,
                  conv_buf, a_buf, b_buf, h_carry, s_buf, s_re, s_im, cnt_carry, ycat):
    R = R_MIX
    halo = (CONV_WIDTH - 1) * BATCH

    @pl.when(pl.program_id(0) == 0)
    def _():
        conv_buf[0:halo, :] = jnp.zeros((halo, D_REC), f32)
        h_carry[...] = jnp.zeros_like(h_carry)
        s_re[...] = jnp.zeros_like(s_re)
        s_im[...] = jnp.zeros_like(s_im)
        cnt_carry[...] = jnp.zeros_like(cnt_carry)

    x = x_ref[...]
    xb = x.astype(bf16)
    gate_br = jnp.dot(xb, w_in_ref[:, 0:D_REC], preferred_element_type=f32)
    rec_br = jnp.dot(xb, w_in_ref[:, D_REC:2 * D_REC], preferred_element_type=f32)
    ssm_br = jnp.dot(xb, w_in_ref[:, 2 * D_REC:], preferred_element_type=f32)

    conv_buf[halo:halo + R, :] = rec_br
    cw = conv_w_ref[...]
    rec = conv_b_ref[...] + cw[3:4, :] * rec_br
    for k in range(CONV_WIDTH - 1):
        rec = rec + cw[k:k + 1, :] * conv_buf[k * BATCH:k * BATCH + R, :]
    conv_buf[0:halo, :] = conv_buf[R:R + halo, :]

    recb = rec.astype(bf16)
    half = D_REC // 2
    g0 = jnp.dot(recb[:, :half], w_gate_ref[0], preferred_element_type=f32)
    g1 = jnp.dot(recb[:, half:], w_gate_ref[1], preferred_element_type=f32)
    r_gate = _sigmoid(jnp.concatenate([g0[:, :half], g1[:, :half]], axis=1) + ba_ref[...])
    i_gate = _sigmoid(jnp.concatenate([g0[:, half:], g1[:, half:]], axis=1) + bx_ref[...])
    z = -lam_ref[...]
    softplus = jnp.maximum(z, 0.0) + jnp.log1p(jnp.exp(-jnp.abs(z)))
    log_a = (-LRU_C) * r_gate * softplus
    a = jnp.exp(log_a)
    mult = jnp.sqrt(-jnp.tanh(log_a) * (1.0 + a * a))
    a_buf[...] = a
    b_buf[...] = mult * (i_gate * rec)

    def lru_step(t, h):
        row = pl.multiple_of(t * BATCH, BATCH)
        h = a_buf[pl.ds(row, BATCH), :] * h + b_buf[pl.ds(row, BATCH), :]
        b_buf[pl.ds(row, BATCH), :] = h
        return h

    h_carry[...] = _unrolled_loop(TL, SCAN_UNROLL, lru_step, h_carry[...])
    y_rec = _gelu(gate_br) * b_buf[...]
    y_rec = y_rec * lax.rsqrt(jnp.mean(y_rec * y_rec, axis=-1, keepdims=True) + RMS_EPS) * g_rec_ref[...]
    ycat[:, 0:D_REC] = y_rec.astype(bf16)

    ub = ssm_br.astype(bf16)
    ys = []
    for c in range(SSM_CHUNKS):
        s_buf[...] = jnp.dot(ub[:, c * CH_W:(c + 1) * CH_W], bbar_ref[c], preferred_element_type=f32)
        ar = abar_re_ref[:, c * ST_W:(c + 1) * ST_W]
        ai = abar_im_ref[:, c * ST_W:(c + 1) * ST_W]

        def ssm_step(t, carry, ar=ar, ai=ai):
            xr, xi = carry
            row = pl.multiple_of(t * BATCH, BATCH)
            bur = s_buf[pl.ds(row, BATCH), 0:ST_W]
            bui = s_buf[pl.ds(row, BATCH), ST_W:2 * ST_W]
            nxr = ar * xr - ai * xi + bur
            nxi = ar * xi + ai * xr + bui
            s_buf[pl.ds(row, BATCH), 0:ST_W] = nxr
            s_buf[pl.ds(row, BATCH), ST_W:2 * ST_W] = nxi
            return nxr, nxi

        xr, xi = _unrolled_loop(TL, SCAN_UNROLL, ssm_step,
                                (s_re[:, c * ST_W:(c + 1) * ST_W], s_im[:, c * ST_W:(c + 1) * ST_W]))
        s_re[:, c * ST_W:(c + 1) * ST_W] = xr
        s_im[:, c * ST_W:(c + 1) * ST_W] = xi
        ys.append(jnp.dot(s_buf[...].astype(bf16), cmat_ref[c], preferred_element_type=f32))
    y = jnp.concatenate(ys, axis=1) + dskip_ref[...] * ssm_br
    y = _gelu(y)
    zz = jnp.dot(y.astype(bf16), w_glu_ref[...], preferred_element_type=f32) + b_glu_ref[...]
    y_ssm = zz[:, :D_SSM] * _sigmoid(zz[:, D_SSM:])
    y_ssm = y_ssm * lax.rsqrt(jnp.mean(y_ssm * y_ssm, axis=-1, keepdims=True) + RMS_EPS) * g_ssm_ref[...]
    ycat[:, D_REC:] = y_ssm.astype(bf16)

    mix = jnp.dot(ycat[...], w_out_ref[...], preferred_element_type=f32)
    v = ALPHA * x + mix
    mu = jnp.mean(v, axis=-1, keepdims=True)
    vc = v - mu
    var = jnp.mean(vc * vc, axis=-1, keepdims=True)
    x1 = vc * lax.rsqrt(var + LN_EPS) * ln_g_ref[...] + ln_b_ref[...]
    x1_ref[...] = x1

    x_hi = x1.astype(bf16)
    x_lo = (x1 - x_hi.astype(f32)).astype(bf16)
    hh = jnp.dot(x_hi, wr_ref[...], preferred_element_type=f32)
    lh = jnp.dot(x_lo, wr_ref[:, 0:ROUTER_W], preferred_element_type=f32)
    logits = hh[:, :ROUTER_W] + hh[:, ROUTER_W:] + lh + rb_ref[...]
    lt = logits.T

    gl = [lt[j:j + 1, :] for j in range(N_EXPERT_GROUPS)]
    gmax = jnp.maximum(jnp.maximum(gl[0], gl[1]), jnp.maximum(gl[2], gl[3]))
    gidx = jnp.where(gl[0] == gmax, 0, jnp.where(gl[1] == gmax, 1, jnp.where(gl[2] == gmax, 2, 3))).astype(i32)
    gsum = (jnp.exp(gl[0] - gmax) + jnp.exp(gl[1] - gmax)) + (jnp.exp(gl[2] - gmax) + jnp.exp(gl[3] - gmax))
    g_top = 1.0 / gsum
    ets = [lt[E_LOGIT_OFF + EXPERTS_PER_GROUP * g:E_LOGIT_OFF + EXPERTS_PER_GROUP * (g + 1), :]
           for g in range(N_EXPERT_GROUPS)]
    e_in = jnp.where(gidx == 0, ets[0], jnp.where(gidx == 1, ets[1], jnp.where(gidx == 2, ets[2], ets[3])))
    sub = lax.broadcasted_iota(i32, (EXPERTS_PER_GROUP, R), 0)
    m1 = jnp.max(e_in, axis=0, keepdims=True)
    i1 = jnp.min(jnp.where(e_in == m1, sub, EXPERTS_PER_GROUP), axis=0, keepdims=True)
    rest = jnp.where(sub == i1, -jnp.inf, e_in)
    m2 = jnp.max(rest, axis=0, keepdims=True)
    i2 = jnp.min(jnp.where(rest == m2, sub, EXPERTS_PER_GROUP), axis=0, keepdims=True)
    p2 = jnp.exp(m2 - m1)
    inv = g_top / (1.0 + p2)
    e1 = gidx * EXPERTS_PER_GROUP + i1
    e2 = gidx * EXPERTS_PER_GROUP + i2

    sub32 = lax.broadcasted_iota(i32, (N_EXPERTS, R), 0)
    oh1 = (sub32 == e1).astype(f32)
    oh2 = (sub32 == e2).astype(f32)
    both = oh1 + oh2
    prefix = jnp.dot(both.astype(bf16), tri_ref[...], preferred_element_type=f32)
    base = prefix + cnt_carry[:, 0:1]
    rank1 = jnp.sum(oh1 * base, axis=0, keepdims=True)
    rank2 = jnp.sum(oh2 * base, axis=0, keepdims=True)
    cnt_carry[...] = cnt_carry[...] + jnp.sum(both, axis=1, keepdims=True)
    cnt_ref[...] = cnt_carry[...]

    eid_ref[...] = jnp.zeros_like(eid_ref)
    eid_ref[0:1, :] = e1
    eid_ref[1:2, :] = e2
    rank_ref[...] = jnp.zeros_like(rank_ref)
    rank_ref[0:1, :] = rank1.astype(i32)
    rank_ref[1:2, :] = rank2.astype(i32)
    gate_ref[...] = jnp.zeros_like(gate_ref)
    gate_ref[0:1, :] = inv
    gate_ref[1:2, :] = inv * p2


def _mixer_call(x, lw):
    R = R_MIX
    halo = (CONV_WIDTH - 1) * BATCH
    n_steps = SEQ // TL
    weights = [lw["w_in"], lw["conv_w"], lw["conv_b"], lw["w_gate"], lw["ba"], lw["bx"], lw["lam"],
               lw["abar_re"], lw["abar_im"], lw["bbar"], lw["cmat"], lw["dskip"], lw["w_glu"], lw["b_glu"],
               lw["g_rec"], lw["g_ssm"], lw["w_out"], lw["ln1_g"], lw["ln1_b"], lw["wr"], lw["rb"], lw["tri"]]
    in_specs = [pl.BlockSpec((R, D_MODEL), lambda i: (i, 0))] + [_full(w.shape) for w in weights]
    out_shape = (
        jax.ShapeDtypeStruct((T_TOK, D_MODEL), f32),
        jax.ShapeDtypeStruct((SUBLANES, T_TOK), i32),
        jax.ShapeDtypeStruct((SUBLANES, T_TOK), f32),
        jax.ShapeDtypeStruct((SUBLANES, T_TOK), i32),
        jax.ShapeDtypeStruct((N_EXPERTS, LANES), f32),
    )
    out_specs = (
        pl.BlockSpec((R, D_MODEL), lambda i: (i, 0)),
        pl.BlockSpec((SUBLANES, R), lambda i: (0, i)),
        pl.BlockSpec((SUBLANES, R), lambda i: (0, i)),
        pl.BlockSpec((SUBLANES, R), lambda i: (0, i)),
        pl.BlockSpec((N_EXPERTS, LANES), lambda i: (0, 0)),
    )
    scratch = [
        pltpu.VMEM((R + halo, D_REC), f32),
        pltpu.VMEM((R, D_REC), f32),
        pltpu.VMEM((R, D_REC), f32),
        pltpu.VMEM((BATCH, D_REC), f32),
        pltpu.VMEM((R, 2 * ST_W), f32),
        pltpu.VMEM((BATCH, SSM_GROUPS * SSM_STATE), f32),
        pltpu.VMEM((BATCH, SSM_GROUPS * SSM_STATE), f32),
        pltpu.VMEM((N_EXPERTS, LANES), f32),
        pltpu.VMEM((R, D_MODEL), bf16),
    ]
    return pl.pallas_call(
        _mixer_kernel, grid=(n_steps,), in_specs=in_specs, out_specs=out_specs, out_shape=out_shape,
        scratch_shapes=scratch, name="mixer",
        compiler_params=pltpu.CompilerParams(dimension_semantics=("arbitrary",), vmem_limit_bytes=VMEM_LIMIT),
    )(x, *weights)


def _dispatch_kernel(dest_ref, x_hbm, xb_in_hbm, xb_hbm, sem):
    del xb_in_hbm
    base = pl.program_id(0) * TD

    def row_copy(src_row, dst_row):
        return pltpu.make_async_copy(x_hbm.at[pl.ds(src_row, 1)], xb_hbm.at[pl.ds(dst_row, 1)], sem)

    def issue(j, c):
        row_copy(base + j, dest_ref[0, 0, j]).start()
        row_copy(base + j, dest_ref[0, 0, TD + j]).start()
        return c

    lax.fori_loop(0, TD, issue, 0)

    def drain(j, c):
        row_copy(0, 0).wait()
        row_copy(0, 0).wait()
        return c

    lax.fori_loop(0, TD, drain, 0)


def _dispatch_call(dest_tiles, x1, xb_zero):
    return pl.pallas_call(
        _dispatch_kernel, grid=(T_TOK // TD,),
        in_specs=[pl.BlockSpec((1, 1, 2 * TD), lambda i: (i, 0, 0), memory_space=pltpu.SMEM),
                  pl.BlockSpec(memory_space=pl.ANY),
                  pl.BlockSpec(memory_space=pl.ANY)],
        out_specs=pl.BlockSpec(memory_space=pl.ANY),
        out_shape=jax.ShapeDtypeStruct((P_ROWS, D_MODEL), f32),
        scratch_shapes=[pltpu.SemaphoreType.DMA(())],
        input_output_aliases={2: 0}, name="dispatch",
        compiler_params=pltpu.CompilerParams(dimension_semantics=("arbitrary",)),
    )(dest_tiles, x1, xb_zero)


def _expert_kernel(be_ref, nb_ref, xb_ref, wg_ref, wu_ref, wd_ref, yb_ref, wg_s, wu_s, wd_s):
    i = pl.program_id(0)
    prev = be_ref[jnp.maximum(i - 1, 0)]

    @pl.when((i == 0) | (be_ref[i] != prev))
    def _():
        wg_s[...] = wg_ref[0].astype(bf16)
        wu_s[...] = wu_ref[0].astype(bf16)
        wd_s[...] = wd_ref[0].astype(bf16)

    @pl.when(i < nb_ref[0])
    def _():
        xb = xb_ref[...].astype(bf16)
        g = jnp.dot(xb, wg_s[...], preferred_element_type=f32)
        u = jnp.dot(xb, wu_s[...], preferred_element_type=f32)
        h = (g * _sigmoid(g)) * u
        yb_ref[...] = jnp.dot(h.astype(bf16), wd_s[...], preferred_element_type=f32)

    @pl.when(i >= nb_ref[0])
    def _():
        yb_ref[...] = jnp.zeros_like(yb_ref)


def _expert_call(block_e, n_used, xb, w_gate, w_up, w_down):
    grid_spec = pltpu.PrefetchScalarGridSpec(
        num_scalar_prefetch=2, grid=(N_BLOCKS,),
        in_specs=[pl.BlockSpec((MOE_BLK, D_MODEL), lambda i, be, nb: (i, 0)),
                  pl.BlockSpec((1, D_MODEL, D_EXPERT), lambda i, be, nb: (be[i], 0, 0)),
                  pl.BlockSpec((1, D_MODEL, D_EXPERT), lambda i, be, nb: (be[i], 0, 0)),
                  pl.BlockSpec((1, D_EXPERT, D_MODEL), lambda i, be, nb: (be[i], 0, 0))],
        out_specs=pl.BlockSpec((MOE_BLK, D_MODEL), lambda i, be, nb: (i, 0)),
        scratch_shapes=[pltpu.VMEM((D_MODEL, D_EXPERT), bf16),
                        pltpu.VMEM((D_MODEL, D_EXPERT), bf16),
                        pltpu.VMEM((D_EXPERT, D_MODEL), bf16)])
    return pl.pallas_call(
        _expert_kernel, grid_spec=grid_spec,
        out_shape=jax.ShapeDtypeStruct((P_ROWS, D_MODEL), f32), name="experts",
        compiler_params=pltpu.CompilerParams(dimension_semantics=("arbitrary",), vmem_limit_bytes=VMEM_LIMIT),
    )(block_e, n_used, xb, w_gate, w_up, w_down)


def _combine_kernel(dest_ref, x1_ref, g_ref, yb_hbm, ln_g_ref, ln_b_ref, out_ref, ybuf, sem):
    def row_copy(src_row, k, j):
        return pltpu.make_async_copy(yb_hbm.at[pl.ds(src_row, 1)], ybuf.at[k, pl.ds(j, 1)], sem)

    def issue(j, c):
        row_copy(dest_ref[0, 0, j], 0, j).start()
        row_copy(dest_ref[0, 0, TC + j], 1, j).start()
        return c

    lax.fori_loop(0, TC, issue, 0)

    def drain(j, c):
        row_copy(0, 0, 0).wait()
        row_copy(0, 1, 0).wait()
        return c

    lax.fori_loop(0, TC, drain, 0)

    g = g_ref[...]
    ff = g[:, 0:1] * ybuf[0] + g[:, 1:2] * ybuf[1]
    v = ALPHA * x1_ref[...] + ff
    mu = jnp.mean(v, axis=-1, keepdims=True)
    vc = v - mu
    var = jnp.mean(vc * vc, axis=-1, keepdims=True)
    out_ref[...] = vc * lax.rsqrt(var + LN_EPS) * ln_g_ref[...] + ln_b_ref[...]


def _combine_call(dest_tiles, x1, gates_t, yb, ln_g, ln_b):
    return pl.pallas_call(
        _combine_kernel, grid=(T_TOK // TC,),
        in_specs=[pl.BlockSpec((1, 1, 2 * TC), lambda i: (i, 0, 0), memory_space=pltpu.SMEM),
                  pl.BlockSpec((TC, D_MODEL), lambda i: (i, 0)),
                  pl.BlockSpec((TC, 2), lambda i: (i, 0)),
                  pl.BlockSpec(memory_space=pl.ANY),
                  _full((1, D_MODEL)), _full((1, D_MODEL))],
        out_specs=pl.BlockSpec((TC, D_MODEL), lambda i: (i, 0)),
        out_shape=jax.ShapeDtypeStruct((T_TOK, D_MODEL), f32),
        scratch_shapes=[pltpu.VMEM((2, TC, D_MODEL), f32), pltpu.SemaphoreType.DMA(())],
        name="combine",
        compiler_params=pltpu.CompilerParams(dimension_semantics=("arbitrary",)),
    )(dest_tiles, x1, gates_t, yb, ln_g, ln_b)


def _tile_dest(dest, tile):
    return jnp.transpose(dest.reshape(2, T_TOK // tile, tile), (1, 0, 2)).reshape(T_TOK // tile, 1, 2 * tile)


def _block_diag(blocks):
    n, a, b = blocks.shape
    eye = jnp.eye(n, dtype=blocks.dtype)
    return jnp.einsum("nab,nm->namb", blocks, eye).reshape(n * a, n * b)


def kernel(x, w_in, conv_w, conv_b, lru_wa, lru_ba, lru_wx, lru_bx, lru_lambda, ssm_lambda_re, ssm_lambda_im, ssm_log_dt, ssm_b_re, ssm_b_im, ssm_c_re, ssm_c_im, ssm_d, w_glu, b_glu, g_rec, g_ssm, w_out, ln1_g, ln1_b, router_wg, router_bg, router_we, router_be, exp_w_gate, exp_w_up, exp_w_down, ln2_g, ln2_b):
    a_re, a_im, bb_re, bb_im = _ssm_prep(ssm_lambda_re, ssm_lambda_im, ssm_log_dt, ssm_b_re, ssm_b_im)
    n_state = SSM_GROUPS * SSM_STATE
    a_re = a_re.reshape(DEPTH, SSM_GROUPS, SSM_GROUP, SSM_STATE)[:, :, 0, :].reshape(DEPTH, 1, n_state)
    a_im = a_im.reshape(DEPTH, SSM_GROUPS, SSM_GROUP, SSM_STATE)[:, :, 0, :].reshape(DEPTH, 1, n_state)
    a_re = jnp.broadcast_to(a_re, (DEPTH, BATCH, n_state))
    a_im = jnp.broadcast_to(a_im, (DEPTH, BATCH, n_state))
    gpc = SSM_GROUPS // SSM_CHUNKS
    tri = (lax.broadcasted_iota(i32, (R_MIX, R_MIX), 0) < lax.broadcasted_iota(i32, (R_MIX, R_MIX), 1)).astype(bf16)
    half = D_REC // 2

    layers = []
    for l in range(DEPTH):
        bre = bb_re.reshape(DEPTH, SSM_CHUNKS, gpc, SSM_GROUP, SSM_STATE)[l]
        bim = bb_im.reshape(DEPTH, SSM_CHUNKS, gpc, SSM_GROUP, SSM_STATE)[l]
        bbar = jnp.stack([jnp.concatenate([_block_diag(bre[c]), _block_diag(bim[c])], axis=1)
                          for c in range(SSM_CHUNKS)]).astype(bf16)
        cre = jnp.transpose(ssm_c_re[l].reshape(SSM_CHUNKS, gpc, SSM_GROUP, SSM_STATE), (0, 1, 3, 2))
        cim = jnp.transpose(ssm_c_im[l].reshape(SSM_CHUNKS, gpc, SSM_GROUP, SSM_STATE), (0, 1, 3, 2))
        cmat = jnp.stack([jnp.concatenate([_block_diag(cre[c]), -_block_diag(cim[c])], axis=0)
                          for c in range(SSM_CHUNKS)]).astype(bf16)
        wa_d = _block_diag(lru_wa[l])
        wx_d = _block_diag(lru_wx[l])
        w_gate = jnp.stack([jnp.concatenate([wa_d[h * half:(h + 1) * half, h * half:(h + 1) * half],
                                             wx_d[h * half:(h + 1) * half, h * half:(h + 1) * half]], axis=1)
                            for h in range(2)]).astype(bf16)
        wr32 = jnp.zeros((D_MODEL, ROUTER_W), f32)
        wr32 = wr32.at[:, 0:N_EXPERT_GROUPS].set(router_wg[l])
        wr32 = wr32.at[:, E_LOGIT_OFF:E_LOGIT_OFF + N_EXPERTS].set(router_we[l])
        wr_hi = wr32.astype(bf16)
        wr_lo = (wr32 - wr_hi.astype(f32)).astype(bf16)
        rb = jnp.zeros((1, ROUTER_W), f32)
        rb = rb.at[0, 0:N_EXPERT_GROUPS].set(router_bg[l])
        rb = rb.at[0, E_LOGIT_OFF:E_LOGIT_OFF + N_EXPERTS].set(router_be[l])
        layers.append(dict(
            w_in=w_in[l].astype(bf16), conv_w=conv_w[l], conv_b=conv_b[l].reshape(1, D_REC),
            w_gate=w_gate, ba=lru_ba[l].reshape(1, D_REC), bx=lru_bx[l].reshape(1, D_REC),
            lam=lru_lambda[l].reshape(1, D_REC), abar_re=a_re[l], abar_im=a_im[l], bbar=bbar, cmat=cmat,
            dskip=ssm_d[l].reshape(1, D_SSM), w_glu=w_glu[l].astype(bf16), b_glu=b_glu[l].reshape(1, 2 * D_SSM),
            g_rec=g_rec[l].reshape(1, D_REC), g_ssm=g_ssm[l].reshape(1, D_SSM), w_out=w_out[l].astype(bf16),
            ln1_g=ln1_g[l].reshape(1, D_MODEL), ln1_b=ln1_b[l].reshape(1, D_MODEL),
            wr=jnp.concatenate([wr_hi, wr_lo], axis=1), rb=rb, tri=tri))

    xt = jnp.transpose(x, (1, 0, 2)).reshape(T_TOK, D_MODEL)
    xb_zero = jnp.zeros((P_ROWS, D_MODEL), f32)
    wg_all = exp_w_gate.reshape(DEPTH * N_EXPERTS, D_MODEL, D_EXPERT)
    wu_all = exp_w_up.reshape(DEPTH * N_EXPERTS, D_MODEL, D_EXPERT)
    wd_all = exp_w_down.reshape(DEPTH * N_EXPERTS, D_EXPERT, D_MODEL)
    for l in range(DEPTH):
        x1, eid, gates, rank, cnt = _mixer_call(xt, layers[l])
        counts = cnt[:, 0].astype(i32)
        padded = (counts + MOE_BLK - 1) // MOE_BLK * MOE_BLK
        pad_ends = jnp.cumsum(padded)
        pad_starts = pad_ends - padded
        dest = pad_starts[eid[0:2]] + rank[0:2]
        n_used = (pad_ends[-1:] // MOE_BLK).astype(i32)
        block_e = jnp.minimum(
            jnp.searchsorted(pad_ends, jnp.arange(N_BLOCKS, dtype=i32) * MOE_BLK, side="right"),
            N_EXPERTS - 1).astype(i32)
        xb = _dispatch_call(_tile_dest(dest, TD), x1, xb_zero)
        yb = _expert_call(block_e + l * N_EXPERTS, n_used, xb, wg_all, wu_all, wd_all)
        xt = _combine_call(_tile_dest(dest, TC), x1, jnp.transpose(gates[0:2]), yb,
                           ln2_g[l].reshape(1, D_MODEL), ln2_b[l].reshape(1, D_MODEL))
    return jnp.transpose(xt.reshape(SEQ, BATCH, D_MODEL), (1, 0, 2))
```

```python
import functools
import math

import jax
import jax.numpy as jnp
from jax import lax
from jax.experimental import pallas as pl
from jax.experimental.pallas import tpu as pltpu

D_MODEL = 1024
BATCH = 8
SEQ = 4096
DEPTH = 4
D_REC = 512
D_SSM = 512
REC_HEADS = 8
REC_HEAD_DIM = 64
CONV_WIDTH = 4
LRU_C = 8.0
SSM_GROUP = 16
SSM_GROUPS = 32
SSM_STATE = 64
N_EXPERT_GROUPS = 4
EXPERTS_PER_GROUP = 8
N_EXPERTS = 32
D_EXPERT = 512
ALPHA = (2.0 * DEPTH) ** 0.25
LN_EPS = 1e-5
RMS_EPS = 1e-6

T_TOK = BATCH * SEQ

SUBLANES = 8
LANES = 128
TL = 32
R_MIX = TL * BATCH
SCAN_UNROLL = 8
SSM_CHUNKS = 4
CH_W = D_SSM // SSM_CHUNKS
ST_W = SSM_GROUPS * SSM_STATE // SSM_CHUNKS
ROUTER_W = 128
E_LOGIT_OFF = 8
MOE_BLK = 256
N_BLOCKS = (T_TOK * 2) // MOE_BLK + N_EXPERTS
P_ROWS = N_BLOCKS * MOE_BLK
TD = 512
TC = 256
TS = 4096
DMA_UNROLL = 8
VMEM_LIMIT = 56 * 1024 * 1024

f32 = jnp.float32
bf16 = jnp.bfloat16
i32 = jnp.int32


def _full(shape):
    n = len(shape)
    return pl.BlockSpec(shape, lambda *_: (0,) * n)


def _sigmoid(v):
    return 1.0 / (1.0 + jnp.exp(-v))


def _gelu(v):
    return 0.5 * v * (1.0 + jnp.tanh(0.7978845608028654 * (v + 0.044715 * (v * v * v))))


def _unrolled_loop(n_steps, unroll, body, init):
    def outer(o, carry):
        for u in range(unroll):
            carry = body(o * unroll + u, carry)
        return carry
    return lax.fori_loop(0, n_steps // unroll, outer, init)


def _ssm_prep_kernel(lre_ref, lim_ref, ldt_ref, bre_ref, bim_ref,
                     are_ref, aim_ref, bbre_ref, bbim_ref):
    lre = jnp.minimum(lre_ref[...], -1e-4)
    lim = lim_ref[...]
    dt = jnp.exp(ldt_ref[...])
    mag = jnp.exp(lre * dt)
    a_re = mag * jnp.cos(lim * dt)
    a_im = mag * jnp.sin(lim * dt)
    den = lre * lre + lim * lim
    p_re = a_re - 1.0
    p_im = a_im
    coef_re = (p_re * lre + p_im * lim) / den
    coef_im = (p_im * lre - p_re * lim) / den
    br = bre_ref[...]
    bi = bim_ref[...]
    are_ref[...] = a_re
    aim_ref[...] = a_im
    bbre_ref[...] = coef_re * br - coef_im * bi
    bbim_ref[...] = coef_re * bi + coef_im * br


def _ssm_prep(lam_re, lam_im, log_dt, b_re, b_im):
    rows = DEPTH * SSM_GROUPS * SSM_GROUP

    def rep(a):
        return jnp.broadcast_to(a[:, :, None, :], (DEPTH, SSM_GROUPS, SSM_GROUP, SSM_STATE)).reshape(rows, SSM_STATE)

    ldt = jnp.broadcast_to(log_dt[:, :, None], (DEPTH, SSM_GROUPS, SSM_STATE))
    bre_t = jnp.transpose(b_re, (0, 1, 3, 2)).reshape(rows, SSM_STATE)
    bim_t = jnp.transpose(b_im, (0, 1, 3, 2)).reshape(rows, SSM_STATE)
    shp = jax.ShapeDtypeStruct((rows, SSM_STATE), f32)
    a_re, a_im, bb_re, bb_im = pl.pallas_call(
        _ssm_prep_kernel, out_shape=(shp, shp, shp, shp), name="ssm_prep",
    )(rep(lam_re), rep(lam_im), rep(ldt), bre_t, bim_t)
    return a_re, a_im, bb_re, bb_im


def _mixer_kernel(x_ref, w_in_ref, conv_w_ref, conv_b_ref, w_gate_ref, ba_ref, bx_ref, lam_ref,
                  abar_re_ref, abar_im_ref, bbar_ref, cmat_ref, dskip_ref, w_glu_ref, b_glu_ref,
                  g_rec_ref, g_ssm_ref, w_out_ref, ln_g_ref, ln_b_ref, wr_ref, rb_ref, tri_ref,
                  x1_ref, eid_ref, gate_ref, rank_ref, cnt_ref,
                  conv_buf, a_buf, b_buf, h_carry, s_buf, s_re, s_im, cnt_carry, ycat):
    R = R_MIX
    halo = (CONV_WIDTH - 1) * BATCH

    @pl.when(pl.program_id(0) == 0)
    def _():
        conv_buf[0:halo, :] = jnp.zeros((halo, D_REC), f32)
        h_carry[...] = jnp.zeros_like(h_carry)
        s_re[...] = jnp.zeros_like(s_re)
        s_im[...] = jnp.zeros_like(s_im)
        cnt_carry[...] = jnp.zeros_like(cnt_carry)

    x = x_ref[...]
    xb = x.astype(bf16)
    gate_br = jnp.dot(xb, w_in_ref[:, 0:D_REC], preferred_element_type=f32)
    rec_br = jnp.dot(xb, w_in_ref[:, D_REC:2 * D_REC], preferred_element_type=f32)
    ssm_br = jnp.dot(xb, w_in_ref[:, 2 * D_REC:], preferred_element_type=f32)

    conv_buf[halo:halo + R, :] = rec_br
    cw = conv_w_ref[...]
    rec = conv_b_ref[...] + cw[3:4, :] * rec_br
    for k in range(CONV_WIDTH - 1):
        rec = rec + cw[k:k + 1, :] * conv_buf[k * BATCH:k * BATCH + R, :]
    conv_buf[0:halo, :] = conv_buf[R:R + halo, :]

    recb = rec.astype(bf16)
    half = D_REC // 2
    g0 = jnp.dot(recb[:, :half], w_gate_ref[0], preferred_element_type=f32)
    g1 = jnp.dot(recb[:, half:], w_gate_ref[1], preferred_element_type=f32)
    r_gate = _sigmoid(jnp.concatenate([g0[:, :half], g1[:, :half]], axis=1) + ba_ref[...])
    i_gate = _sigmoid(jnp.concatenate([g0[:, half:], g1[:, half:]], axis=1) + bx_ref[...])
    z = -lam_ref[...]
    softplus = jnp.maximum(z, 0.0) + jnp.log1p(jnp.exp(-jnp.abs(z)))
    log_a = (-LRU_C) * r_gate * softplus
    a = jnp.exp(log_a)
    mult = jnp.sqrt(-jnp.tanh(log_a) * (1.0 + a * a))
    a_buf[...] = a
    b_buf[...] = mult * (i_gate * rec)

    def lru_step(t, h):
        row = pl.multiple_of(t * BATCH, BATCH)
        h = a_buf[pl.ds(row, BATCH), :] * h + b_buf[pl.ds(row, BATCH), :]
        b_buf[pl.ds(row, BATCH), :] = h
        return h

    h_carry[...] = _unrolled_loop(TL, SCAN_UNROLL, lru_step, h_carry[...])
    y_rec = _gelu(gate_br) * b_buf[...]
    y_rec = y_rec * lax.rsqrt(jnp.mean(y_rec * y_rec, axis=-1, keepdims=True) + RMS_EPS) * g_rec_ref[...]
    ycat[:, 0:D_REC] = y_rec.astype(bf16)

    ub = ssm_br.astype(bf16)
    ys = []
    for c in range(SSM_CHUNKS):
        s_buf[...] = jnp.dot(ub[:, c * CH_W:(c + 1) * CH_W], bbar_ref[c], preferred_element_type=f32)
        ar = abar_re_ref[:, c * ST_W:(c + 1) * ST_W]
        ai = abar_im_ref[:, c * ST_W:(c + 1) * ST_W]

        def ssm_step(t, carry, ar=ar, ai=ai):
            xr, xi = carry
            row = pl.multiple_of(t * BATCH, BATCH)
            bur = s_buf[pl.ds(row, BATCH), 0:ST_W]
            bui = s_buf[pl.ds(row, BATCH), ST_W:2 * ST_W]
            nxr = ar * xr - ai * xi + bur
            nxi = ar * xi + ai * xr + bui
            s_buf[pl.ds(row, BATCH), 0:ST_W] = nxr
            s_buf[pl.ds(row, BATCH), ST_W:2 * ST_W] = nxi
            return nxr, nxi

        xr, xi = _unrolled_loop(TL, SCAN_UNROLL, ssm_step,
                                (s_re[:, c * ST_W:(c + 1) * ST_W], s_im[:, c * ST_W:(c + 1) * ST_W]))
        s_re[:, c * ST_W:(c + 1) * ST_W] = xr
        s_im[:, c * ST_W:(c + 1) * ST_W] = xi
        ys.append(jnp.dot(s_buf[...].astype(bf16), cmat_ref[c], preferred_element_type=f32))
    y = jnp.concatenate(ys, axis=1) + dskip_ref[...] * ssm_br
    y = _gelu(y)
    zz = jnp.dot(y.astype(bf16), w_glu_ref[...], preferred_element_type=f32) + b_glu_ref[...]
    y_ssm = zz[:, :D_SSM] * _sigmoid(zz[:, D_SSM:])
    y_ssm = y_ssm * lax.rsqrt(jnp.mean(y_ssm * y_ssm, axis=-1, keepdims=True) + RMS_EPS) * g_ssm_ref[...]
    ycat[:, D_REC:] = y_ssm.astype(bf16)

    mix = jnp.dot(ycat[...], w_out_ref[...], preferred_element_type=f32)
    v = ALPHA * x + mix
    mu = jnp.mean(v, axis=-1, keepdims=True)
    vc = v - mu
    var = jnp.mean(vc * vc, axis=-1, keepdims=True)
    x1 = vc * lax.rsqrt(var + LN_EPS) * ln_g_ref[...] + ln_b_ref[...]
    x1_ref[...] = x1

    x_hi = x1.astype(bf16)
    x_lo = (x1 - x_hi.astype(f32)).astype(bf16)
    hh = jnp.dot(x_hi, wr_ref[...], preferred_element_type=f32)
    lh = jnp.dot(x_lo, wr_ref[:, 0:ROUTER_W], preferred_element_type=f32)
    logits = hh[:, :ROUTER_W] + hh[:, ROUTER_W:] + lh + rb_ref[...]
    lt = logits.T

    gl = [lt[j:j + 1, :] for j in range(N_EXPERT_GROUPS)]
    gmax = jnp.maximum(jnp.maximum(gl[0], gl[1]), jnp.maximum(gl[2], gl[3]))
    gidx = jnp.where(gl[0] == gmax, 0, jnp.where(gl[1] == gmax, 1, jnp.where(gl[2] == gmax, 2, 3))).astype(i32)
    gsum = (jnp.exp(gl[0] - gmax) + jnp.exp(gl[1] - gmax)) + (jnp.exp(gl[2] - gmax) + jnp.exp(gl[3] - gmax))
    g_top = 1.0 / gsum
    ets = [lt[E_LOGIT_OFF + EXPERTS_PER_GROUP * g:E_LOGIT_OFF + EXPERTS_PER_GROUP * (g + 1), :]
           for g in range(N_EXPERT_GROUPS)]
    e_in = jnp.where(gidx == 0, ets[0], jnp.where(gidx == 1, ets[1], jnp.where(gidx == 2, ets[2], ets[3])))
    sub = lax.broadcasted_iota(i32, (EXPERTS_PER_GROUP, R), 0)
    m1 = jnp.max(e_in, axis=0, keepdims=True)
    i1 = jnp.min(jnp.where(e_in == m1, sub, EXPERTS_PER_GROUP), axis=0, keepdims=True)
    rest = jnp.where(sub == i1, -jnp.inf, e_in)
    m2 = jnp.max(rest, axis=0, keepdims=True)
    i2 = jnp.min(jnp.where(rest == m2, sub, EXPERTS_PER_GROUP), axis=0, keepdims=True)
    p2 = jnp.exp(m2 - m1)
    inv = g_top / (1.0 + p2)
    e1 = gidx * EXPERTS_PER_GROUP + i1
    e2 = gidx * EXPERTS_PER_GROUP + i2

    sub32 = lax.broadcasted_iota(i32, (N_EXPERTS, R), 0)
    oh1 = (sub32 == e1).astype(f32)
    oh2 = (sub32 == e2).astype(f32)
    both = oh1 + oh2
    prefix = jnp.dot(both.astype(bf16), tri_ref[...], preferred_element_type=f32)
    base = prefix + cnt_carry[:, 0:1]
    rank1 = jnp.sum(oh1 * base, axis=0, keepdims=True)
    rank2 = jnp.sum(oh2 * base, axis=0, keepdims=True)
    cnt_carry[...] = cnt_carry[...] + jnp.sum(both, axis=1, keepdims=True)
    cnt_ref[...] = cnt_carry[...]

    eid_ref[...] = jnp.zeros_like(eid_ref)
    eid_ref[0:1, :] = e1
    eid_ref[1:2, :] = e2
    rank_ref[...] = jnp.zeros_like(rank_ref)
    rank_ref[0:1, :] = rank1.astype(i32)
    rank_ref[1:2, :] = rank2.astype(i32)
    gate_ref[...] = jnp.zeros_like(gate_ref)
    gate_ref[0:1, :] = inv
    gate_ref[1:2, :] = inv * p2


def _mixer_call(x, lw):
    R = R_MIX
    halo = (CONV_WIDTH - 1) * BATCH
    n_steps = SEQ // TL
    weights = [lw["w_in"], lw["conv_w"], lw["conv_b"], lw["w_gate"], lw["ba"], lw["bx"], lw["lam"],
               lw["abar_re"], lw["abar_im"], lw["bbar"], lw["cmat"], lw["dskip"], lw["w_glu"], lw["b_glu"],
               lw["g_rec"], lw["g_ssm"], lw["w_out"], lw["ln1_g"], lw["ln1_b"], lw["wr"], lw["rb"], lw["tri"]]
    in_specs = [pl.BlockSpec((R, D_MODEL), lambda i: (i, 0))] + [_full(w.shape) for w in weights]
    out_shape = (
        jax.ShapeDtypeStruct((T_TOK, D_MODEL), f32),
        jax.ShapeDtypeStruct((SUBLANES, T_TOK), i32),
        jax.ShapeDtypeStruct((SUBLANES, T_TOK), f32),
        jax.ShapeDtypeStruct((SUBLANES, T_TOK), i32),
        jax.ShapeDtypeStruct((N_EXPERTS, LANES), f32),
    )
    out_specs = (
        pl.BlockSpec((R, D_MODEL), lambda i: (i, 0)),
        pl.BlockSpec((SUBLANES, R), lambda i: (0, i)),
        pl.BlockSpec((SUBLANES, R), lambda i: (0, i)),
        pl.BlockSpec((SUBLANES, R), lambda i: (0, i)),
        pl.BlockSpec((N_EXPERTS, LANES), lambda i: (0, 0)),
    )
    scratch = [
        pltpu.VMEM((R + halo, D_REC), f32),
        pltpu.VMEM((R, D_REC), f32),
        pltpu.VMEM((R, D_REC), f32),
        pltpu.VMEM((BATCH, D_REC), f32),
        pltpu.VMEM((R, 2 * ST_W), f32),
        pltpu.VMEM((BATCH, SSM_GROUPS * SSM_STATE), f32),
        pltpu.VMEM((BATCH, SSM_GROUPS * SSM_STATE), f32),
        pltpu.VMEM((N_EXPERTS, LANES), f32),
        pltpu.VMEM((R, D_MODEL), bf16),
    ]
    return pl.pallas_call(
        _mixer_kernel, grid=(n_steps,), in_specs=in_specs, out_specs=out_specs, out_shape=out_shape,
        scratch_shapes=scratch, name="mixer",
        compiler_params=pltpu.CompilerParams(dimension_semantics=("arbitrary",), vmem_limit_bytes=VMEM_LIMIT),
    )(x, *weights)


def _dispatch_kernel(dest_ref, x_ref, xb_in_hbm, xb_hbm, sem):
    del xb_in_hbm

    def issue(j, c):
        for k in range(2):
            pltpu.make_async_copy(x_ref.at[pl.ds(j, 1)], xb_hbm.at[pl.ds(dest_ref[0, 0, k * TD + j], 1)], sem).start()
        return c

    _unrolled_loop(TD, DMA_UNROLL, issue, 0)
    for k in range(2):
        pltpu.make_async_copy(x_ref, xb_hbm.at[pl.ds(0, TD)], sem).wait()


def _dispatch_call(dest_tiles, x1, xb_zero):
    return pl.pallas_call(
        _dispatch_kernel, grid=(T_TOK // TD,),
        in_specs=[pl.BlockSpec((1, 1, 2 * TD), lambda i: (i, 0, 0), memory_space=pltpu.SMEM),
                  pl.BlockSpec((TD, D_MODEL), lambda i: (i, 0)),
                  pl.BlockSpec(memory_space=pl.ANY)],
        out_specs=pl.BlockSpec(memory_space=pl.ANY),
        out_shape=jax.ShapeDtypeStruct((P_ROWS, D_MODEL), f32),
        scratch_shapes=[pltpu.SemaphoreType.DMA(())],
        input_output_aliases={2: 0}, name="dispatch",
        compiler_params=pltpu.CompilerParams(dimension_semantics=("arbitrary",)),
    )(dest_tiles, x1, xb_zero)


def _expert_kernel(be_ref, nb_ref, xb_ref, wg_ref, wu_ref, wd_ref, yb_ref, wg_s, wu_s, wd_s):
    i = pl.program_id(0)
    prev = be_ref[jnp.maximum(i - 1, 0)]

    @pl.when((i == 0) | (be_ref[i] != prev))
    def _():
        wg_s[...] = wg_ref[0].astype(bf16)
        wu_s[...] = wu_ref[0].astype(bf16)
        wd_s[...] = wd_ref[0].astype(bf16)

    @pl.when(i < nb_ref[0])
    def _():
        xb = xb_ref[...].astype(bf16)
        g = jnp.dot(xb, wg_s[...], preferred_element_type=f32)
        u = jnp.dot(xb, wu_s[...], preferred_element_type=f32)
        h = (g * _sigmoid(g)) * u
        yb_ref[...] = jnp.dot(h.astype(bf16), wd_s[...], preferred_element_type=f32)

    @pl.when(i >= nb_ref[0])
    def _():
        yb_ref[...] = jnp.zeros_like(yb_ref)


def _expert_call(block_e, n_used, xb, w_gate, w_up, w_down):
    grid_spec = pltpu.PrefetchScalarGridSpec(
        num_scalar_prefetch=2, grid=(N_BLOCKS,),
        in_specs=[pl.BlockSpec((MOE_BLK, D_MODEL), lambda i, be, nb: (i, 0)),
                  pl.BlockSpec((1, D_MODEL, D_EXPERT), lambda i, be, nb: (be[i], 0, 0)),
                  pl.BlockSpec((1, D_MODEL, D_EXPERT), lambda i, be, nb: (be[i], 0, 0)),
                  pl.BlockSpec((1, D_EXPERT, D_MODEL), lambda i, be, nb: (be[i], 0, 0))],
        out_specs=pl.BlockSpec((MOE_BLK, D_MODEL), lambda i, be, nb: (i, 0)),
        scratch_shapes=[pltpu.VMEM((D_MODEL, D_EXPERT), bf16),
                        pltpu.VMEM((D_MODEL, D_EXPERT), bf16),
                        pltpu.VMEM((D_EXPERT, D_MODEL), bf16)])
    return pl.pallas_call(
        _expert_kernel, grid_spec=grid_spec,
        out_shape=jax.ShapeDtypeStruct((P_ROWS, D_MODEL), f32), name="experts",
        compiler_params=pltpu.CompilerParams(dimension_semantics=("arbitrary",), vmem_limit_bytes=VMEM_LIMIT),
    )(block_e, n_used, xb, w_gate, w_up, w_down)


def _combine_kernel(dest_ref, x1_ref, g_ref, yb_hbm, ln_g_ref, ln_b_ref, out_ref, ybuf, sem):
    def issue(j, c):
        for k in range(2):
            pltpu.make_async_copy(yb_hbm.at[pl.ds(dest_ref[0, 0, k * TC + j], 1)], ybuf.at[k, pl.ds(j, 1)], sem).start()
        return c

    _unrolled_loop(TC, DMA_UNROLL, issue, 0)
    for k in range(2):
        pltpu.make_async_copy(yb_hbm.at[pl.ds(0, TC)], ybuf.at[k], sem).wait()

    g = g_ref[...]
    ff = g[:, 0:1] * ybuf[0] + g[:, 1:2] * ybuf[1]
    v = ALPHA * x1_ref[...] + ff
    mu = jnp.mean(v, axis=-1, keepdims=True)
    vc = v - mu
    var = jnp.mean(vc * vc, axis=-1, keepdims=True)
    out_ref[...] = vc * lax.rsqrt(var + LN_EPS) * ln_g_ref[...] + ln_b_ref[...]


def _combine_call(dest_tiles, x1, gates_t, yb, ln_g, ln_b):
    return pl.pallas_call(
        _combine_kernel, grid=(T_TOK // TC,),
        in_specs=[pl.BlockSpec((1, 1, 2 * TC), lambda i: (i, 0, 0), memory_space=pltpu.SMEM),
                  pl.BlockSpec((TC, D_MODEL), lambda i: (i, 0)),
                  pl.BlockSpec((TC, 2), lambda i: (i, 0)),
                  pl.BlockSpec(memory_space=pl.ANY),
                  _full((1, D_MODEL)), _full((1, D_MODEL))],
        out_specs=pl.BlockSpec((TC, D_MODEL), lambda i: (i, 0)),
        out_shape=jax.ShapeDtypeStruct((T_TOK, D_MODEL), f32),
        scratch_shapes=[pltpu.VMEM((2, TC, D_MODEL), f32), pltpu.SemaphoreType.DMA(())],
        name="combine",
        compiler_params=pltpu.CompilerParams(dimension_semantics=("arbitrary",)),
    )(dest_tiles, x1, gates_t, yb, ln_g, ln_b)


def _slot_kernel(eid_ref, rank_ref, start_ref, dest_ref):
    sub = lax.broadcasted_iota(i32, (N_EXPERTS, TS), 0)
    starts = start_ref[...]
    dest_ref[...] = rank_ref[...]
    for k in range(2):
        e = eid_ref[k:k + 1, :]
        start = jnp.sum(jnp.where(sub == e, starts, 0), axis=0, keepdims=True)
        dest_ref[k:k + 1, :] = start + rank_ref[k:k + 1, :]


def _slot_call(eid, rank, pad_starts):
    return pl.pallas_call(
        _slot_kernel, grid=(T_TOK // TS,),
        in_specs=[pl.BlockSpec((SUBLANES, TS), lambda i: (0, i)),
                  pl.BlockSpec((SUBLANES, TS), lambda i: (0, i)),
                  _full((N_EXPERTS, 1))],
        out_specs=pl.BlockSpec((SUBLANES, TS), lambda i: (0, i)),
        out_shape=jax.ShapeDtypeStruct((SUBLANES, T_TOK), i32), name="slots",
    )(eid, rank, pad_starts.reshape(N_EXPERTS, 1))


def _tile_dest(dest, tile):
    return jnp.transpose(dest.reshape(2, T_TOK // tile, tile), (1, 0, 2)).reshape(T_TOK // tile, 1, 2 * tile)


def _block_diag(blocks):
    n, a, b = blocks.shape
    eye = jnp.eye(n, dtype=blocks.dtype)
    return jnp.einsum("nab,nm->namb", blocks, eye).reshape(n * a, n * b)


def kernel(x, w_in, conv_w, conv_b, lru_wa, lru_ba, lru_wx, lru_bx, lru_lambda, ssm_lambda_re, ssm_lambda_im, ssm_log_dt, ssm_b_re, ssm_b_im, ssm_c_re, ssm_c_im, ssm_d, w_glu, b_glu, g_rec, g_ssm, w_out, ln1_g, ln1_b, router_wg, router_bg, router_we, router_be, exp_w_gate, exp_w_up, exp_w_down, ln2_g, ln2_b):
    a_re, a_im, bb_re, bb_im = _ssm_prep(ssm_lambda_re, ssm_lambda_im, ssm_log_dt, ssm_b_re, ssm_b_im)
    n_state = SSM_GROUPS * SSM_STATE
    a_re = a_re.reshape(DEPTH, SSM_GROUPS, SSM_GROUP, SSM_STATE)[:, :, 0, :].reshape(DEPTH, 1, n_state)
    a_im = a_im.reshape(DEPTH, SSM_GROUPS, SSM_GROUP, SSM_STATE)[:, :, 0, :].reshape(DEPTH, 1, n_state)
    a_re = jnp.broadcast_to(a_re, (DEPTH, BATCH, n_state))
    a_im = jnp.broadcast_to(a_im, (DEPTH, BATCH, n_state))
    gpc = SSM_GROUPS // SSM_CHUNKS
    tri = (lax.broadcasted_iota(i32, (R_MIX, R_MIX), 0) < lax.broadcasted_iota(i32, (R_MIX, R_MIX), 1)).astype(bf16)
    half = D_REC // 2

    layers = []
    for l in range(DEPTH):
        bre = bb_re.reshape(DEPTH, SSM_CHUNKS, gpc, SSM_GROUP, SSM_STATE)[l]
        bim = bb_im.reshape(DEPTH, SSM_CHUNKS, gpc, SSM_GROUP, SSM_STATE)[l]
        bbar = jnp.stack([jnp.concatenate([_block_diag(bre[c]), _block_diag(bim[c])], axis=1)
                          for c in range(SSM_CHUNKS)]).astype(bf16)
        cre = jnp.transpose(ssm_c_re[l].reshape(SSM_CHUNKS, gpc, SSM_GROUP, SSM_STATE), (0, 1, 3, 2))
        cim = jnp.transpose(ssm_c_im[l].reshape(SSM_CHUNKS, gpc, SSM_GROUP, SSM_STATE), (0, 1, 3, 2))
        cmat = jnp.stack([jnp.concatenate([_block_diag(cre[c]), -_block_diag(cim[c])], axis=0)
                          for c in range(SSM_CHUNKS)]).astype(bf16)
        wa_d = _block_diag(lru_wa[l])
        wx_d = _block_diag(lru_wx[l])
        w_gate = jnp.stack([jnp.concatenate([wa_d[h * half:(h + 1) * half, h * half:(h + 1) * half],
                                             wx_d[h * half:(h + 1) * half, h * half:(h + 1) * half]], axis=1)
                            for h in range(2)]).astype(bf16)
        wr32 = jnp.zeros((D_MODEL, ROUTER_W), f32)
        wr32 = wr32.at[:, 0:N_EXPERT_GROUPS].set(router_wg[l])
        wr32 = wr32.at[:, E_LOGIT_OFF:E_LOGIT_OFF + N_EXPERTS].set(router_we[l])
        wr_hi = wr32.astype(bf16)
        wr_lo = (wr32 - wr_hi.astype(f32)).astype(bf16)
        rb = jnp.zeros((1, ROUTER_W), f32)
        rb = rb.at[0, 0:N_EXPERT_GROUPS].set(router_bg[l])
        rb = rb.at[0, E_LOGIT_OFF:E_LOGIT_OFF + N_EXPERTS].set(router_be[l])
        layers.append(dict(
            w_in=w_in[l].astype(bf16), conv_w=conv_w[l], conv_b=conv_b[l].reshape(1, D_REC),
            w_gate=w_gate, ba=lru_ba[l].reshape(1, D_REC), bx=lru_bx[l].reshape(1, D_REC),
            lam=lru_lambda[l].reshape(1, D_REC), abar_re=a_re[l], abar_im=a_im[l], bbar=bbar, cmat=cmat,
            dskip=ssm_d[l].reshape(1, D_SSM), w_glu=w_glu[l].astype(bf16), b_glu=b_glu[l].reshape(1, 2 * D_SSM),
            g_rec=g_rec[l].reshape(1, D_REC), g_ssm=g_ssm[l].reshape(1, D_SSM), w_out=w_out[l].astype(bf16),
            ln1_g=ln1_g[l].reshape(1, D_MODEL), ln1_b=ln1_b[l].reshape(1, D_MODEL),
            wr=jnp.concatenate([wr_hi, wr_lo], axis=1), rb=rb, tri=tri))

    xt = jnp.transpose(x, (1, 0, 2)).reshape(T_TOK, D_MODEL)
    xb_zero = jnp.zeros((P_ROWS, D_MODEL), f32)
    wg_all = exp_w_gate.reshape(DEPTH * N_EXPERTS, D_MODEL, D_EXPERT)
    wu_all = exp_w_up.reshape(DEPTH * N_EXPERTS, D_MODEL, D_EXPERT)
    wd_all = exp_w_down.reshape(DEPTH * N_EXPERTS, D_EXPERT, D_MODEL)
    for l in range(DEPTH):
        x1, eid, gates, rank, cnt = _mixer_call(xt, layers[l])
        counts = cnt[:, 0].astype(i32)
        padded = (counts + MOE_BLK - 1) // MOE_BLK * MOE_BLK
        pad_ends = jnp.cumsum(padded)
        pad_starts = pad_ends - padded
        dest = _slot_call(eid, rank, pad_starts)[0:2]
        n_used = (pad_ends[-1:] // MOE_BLK).astype(i32)
        block_row0 = jnp.arange(N_BLOCKS, dtype=i32) * MOE_BLK
        block_e = jnp.minimum(jnp.sum((pad_ends[None, :] <= block_row0[:, None]).astype(i32), axis=1),
                              N_EXPERTS - 1)
        xb = _dispatch_call(_tile_dest(dest, TD), x1, xb_zero)
        yb = _expert_call(block_e + l * N_EXPERTS, n_used, xb, wg_all, wu_all, wd_all)
        xt = _combine_call(_tile_dest(dest, TC), x1, jnp.transpose(gates[0:2]), yb,
                           ln2_g[l].reshape(1, D_MODEL), ln2_b[l].reshape(1, D_MODEL))
    return jnp.transpose(xt.reshape(SEQ, BATCH, D_MODEL), (1, 0, 2))
```

```python
import functools
import math

import jax
import jax.numpy as jnp
from jax import lax
from jax.experimental import pallas as pl
from jax.experimental.pallas import tpu as pltpu

D_MODEL = 1024
BATCH = 8
SEQ = 4096
DEPTH = 4
D_REC = 512
D_SSM = 512
REC_HEADS = 8
REC_HEAD_DIM = 64
CONV_WIDTH = 4
LRU_C = 8.0
SSM_GROUP = 16
SSM_GROUPS = 32
SSM_STATE = 64
N_EXPERT_GROUPS = 4
EXPERTS_PER_GROUP = 8
N_EXPERTS = 32
D_EXPERT = 512
ALPHA = (2.0 * DEPTH) ** 0.25
LN_EPS = 1e-5
RMS_EPS = 1e-6

T_TOK = BATCH * SEQ

SUBLANES = 8
LANES = 128
TL = 64
R_MIX = TL * BATCH
SCAN_UNROLL = TL
SSM_CHUNKS = 4
CH_W = D_SSM // SSM_CHUNKS
ST_W = SSM_GROUPS * SSM_STATE // SSM_CHUNKS
ROUTER_W = 128
E_LOGIT_OFF = 8
MOE_BLK = 512
MOE_SUB = MOE_BLK // 2
N_BLOCKS = (T_TOK * 2) // MOE_BLK + N_EXPERTS
P_ROWS = N_BLOCKS * MOE_BLK
TD = 512
TC = R_MIX
TS = 4096
DMA_UNROLL = 8
VMEM_LIMIT = 56 * 1024 * 1024

f32 = jnp.float32
bf16 = jnp.bfloat16
i32 = jnp.int32


def _full(shape):
    n = len(shape)
    return pl.BlockSpec(shape, lambda *_: (0,) * n)


def _sigmoid(v):
    return 1.0 / (1.0 + jnp.exp(-v))


def _gelu(v):
    return 0.5 * v * (1.0 + jnp.tanh(0.7978845608028654 * (v + 0.044715 * (v * v * v))))


def _slab(t):
    return t * BATCH if isinstance(t, int) else pl.multiple_of(t * BATCH, BATCH)


def _unrolled_loop(n_steps, unroll, body, init):
    if unroll >= n_steps:
        carry = init
        for t in range(n_steps):
            carry = body(t, carry)
        return carry

    def outer(o, carry):
        for u in range(unroll):
            carry = body(o * unroll + u, carry)
        return carry
    return lax.fori_loop(0, n_steps // unroll, outer, init)


def _ssm_prep_kernel(lre_ref, lim_ref, ldt_ref, bre_ref, bim_ref,
                     are_ref, aim_ref, bbre_ref, bbim_ref):
    lre = jnp.minimum(lre_ref[...], -1e-4)
    lim = lim_ref[...]
    dt = jnp.exp(ldt_ref[...])
    mag = jnp.exp(lre * dt)
    a_re = mag * jnp.cos(lim * dt)
    a_im = mag * jnp.sin(lim * dt)
    den = lre * lre + lim * lim
    p_re = a_re - 1.0
    p_im = a_im
    coef_re = (p_re * lre + p_im * lim) / den
    coef_im = (p_im * lre - p_re * lim) / den
    br = bre_ref[...]
    bi = bim_ref[...]
    are_ref[...] = a_re
    aim_ref[...] = a_im
    bbre_ref[...] = coef_re * br - coef_im * bi
    bbim_ref[...] = coef_re * bi + coef_im * br


def _ssm_prep(lam_re, lam_im, log_dt, b_re, b_im):
    rows = DEPTH * SSM_GROUPS * SSM_GROUP

    def rep(a):
        return jnp.broadcast_to(a[:, :, None, :], (DEPTH, SSM_GROUPS, SSM_GROUP, SSM_STATE)).reshape(rows, SSM_STATE)

    ldt = jnp.broadcast_to(log_dt[:, :, None], (DEPTH, SSM_GROUPS, SSM_STATE))
    bre_t = jnp.transpose(b_re, (0, 1, 3, 2)).reshape(rows, SSM_STATE)
    bim_t = jnp.transpose(b_im, (0, 1, 3, 2)).reshape(rows, SSM_STATE)
    shp = jax.ShapeDtypeStruct((rows, SSM_STATE), f32)
    a_re, a_im, bb_re, bb_im = pl.pallas_call(
        _ssm_prep_kernel, out_shape=(shp, shp, shp, shp), name="ssm_prep",
    )(rep(lam_re), rep(lam_im), rep(ldt), bre_t, bim_t)
    return a_re, a_im, bb_re, bb_im


def _issue_row_gather(dref, yb_hbm, ybuf, sems, s, unroll):
    def issue(j, c):
        for k in range(2):
            pltpu.make_async_copy(yb_hbm.at[pl.ds(dref[0, 0, k * TC + j], 1)],
                                  ybuf.at[s, k, pl.ds(j, 1)], sems.at[s]).start()
        return c
    _unrolled_loop(TC, unroll, issue, 0)


def _wait_row_gather(yb_hbm, ybuf, sems, s):
    for k in range(2):
        pltpu.make_async_copy(yb_hbm.at[pl.ds(0, TC)], ybuf.at[s, k], sems.at[s]).wait()


def _combine_rows(x1, g, y0, y1, ln_g, ln_b):
    v = ALPHA * x1 + (g[:, 0:1] * y0 + g[:, 1:2] * y1)
    mu = jnp.mean(v, axis=-1, keepdims=True)
    vc = v - mu
    var = jnp.mean(vc * vc, axis=-1, keepdims=True)
    return vc * lax.rsqrt(var + LN_EPS) * ln_g + ln_b


def _mixer_kernel(fused_in, *refs):
    if fused_in:
        dest_ref, dest_next_ref, xprev_ref, g_ref, yb_hbm, ln2_g_ref, ln2_b_ref = refs[:7]
        refs = refs[7:]
        ybuf, sems = refs[-2:]
        refs = refs[:-2]
    else:
        x_ref = refs[0]
        refs = refs[1:]
    (w_in_ref, conv_w_ref, conv_b_ref, w_gate_ref, ba_ref, bx_ref, lam_ref,
     abar_re_ref, abar_im_ref, bbar_ref, cmat_ref, dskip_ref, w_glu_ref, b_glu_ref,
     g_rec_ref, g_ssm_ref, w_out_ref, ln_g_ref, ln_b_ref, wr_ref, rb_ref, tri_ref,
     x1_ref, eid_ref, gate_ref, rank_ref, cnt_ref,
     conv_buf, a_buf, b_buf, h_carry, s_buf, s_re, s_im, cnt_carry, ycat) = refs
    R = R_MIX
    halo = (CONV_WIDTH - 1) * BATCH
    step = pl.program_id(0)

    @pl.when(pl.program_id(0) == 0)
    def _():
        conv_buf[0:halo, :] = jnp.zeros((halo, D_REC), f32)
        h_carry[...] = jnp.zeros_like(h_carry)
        s_re[...] = jnp.zeros_like(s_re)
        s_im[...] = jnp.zeros_like(s_im)
        cnt_carry[...] = jnp.zeros_like(cnt_carry)

    if fused_in:
        slot = step % 2

        @pl.when(step == 0)
        def _():
            _issue_row_gather(dest_ref, yb_hbm, ybuf, sems, 0, DMA_UNROLL)

        _wait_row_gather(yb_hbm, ybuf, sems, slot)
        x = _combine_rows(xprev_ref[...], g_ref[...], ybuf[slot, 0], ybuf[slot, 1], ln2_g_ref[...], ln2_b_ref[...])
        _issue_row_gather(dest_next_ref, yb_hbm, ybuf, sems, 1 - slot, TC)
    else:
        x = jnp.swapaxes(x_ref[...], 0, 1).reshape(R, D_MODEL)
    xb = x.astype(bf16)
    gate_br = jnp.dot(xb, w_in_ref[:, 0:D_REC], preferred_element_type=f32)
    rec_br = jnp.dot(xb, w_in_ref[:, D_REC:2 * D_REC], preferred_element_type=f32)
    ssm_br = jnp.dot(xb, w_in_ref[:, 2 * D_REC:], preferred_element_type=f32)

    conv_buf[halo:halo + R, :] = rec_br
    cw = conv_w_ref[...]
    rec = conv_b_ref[...] + cw[3:4, :] * rec_br
    for k in range(CONV_WIDTH - 1):
        rec = rec + cw[k:k + 1, :] * conv_buf[k * BATCH:k * BATCH + R, :]
    conv_buf[0:halo, :] = conv_buf[R:R + halo, :]

    recb = rec.astype(bf16)
    half = D_REC // 2
    g0 = jnp.dot(recb[:, :half], w_gate_ref[0], preferred_element_type=f32)
    g1 = jnp.dot(recb[:, half:], w_gate_ref[1], preferred_element_type=f32)
    r_gate = _sigmoid(jnp.concatenate([g0[:, :half], g1[:, :half]], axis=1) + ba_ref[...])
    i_gate = _sigmoid(jnp.concatenate([g0[:, half:], g1[:, half:]], axis=1) + bx_ref[...])
    z = -lam_ref[...]
    softplus = jnp.maximum(z, 0.0) + jnp.log1p(jnp.exp(-jnp.abs(z)))
    log_a = (-LRU_C) * r_gate * softplus
    a = jnp.exp(log_a)
    mult = jnp.sqrt(-jnp.tanh(log_a) * (1.0 + a * a))
    a_buf[...] = a
    b_buf[...] = mult * (i_gate * rec)

    def lru_step(t, h):
        row = _slab(t)
        h = a_buf[pl.ds(row, BATCH), :] * h + b_buf[pl.ds(row, BATCH), :]
        b_buf[pl.ds(row, BATCH), :] = h
        return h

    h_carry[...] = _unrolled_loop(TL, SCAN_UNROLL, lru_step, h_carry[...])
    y_rec = _gelu(gate_br) * b_buf[...]
    y_rec = y_rec * lax.rsqrt(jnp.mean(y_rec * y_rec, axis=-1, keepdims=True) + RMS_EPS) * g_rec_ref[...]
    ycat[:, 0:D_REC] = y_rec.astype(bf16)

    ub = ssm_br.astype(bf16)
    ys = []
    for c in range(SSM_CHUNKS):
        sb = s_buf.at[c]
        sb[...] = jnp.dot(ub[:, c * CH_W:(c + 1) * CH_W], bbar_ref[c], preferred_element_type=f32)
        ar = abar_re_ref[:, c * ST_W:(c + 1) * ST_W]
        ai = abar_im_ref[:, c * ST_W:(c + 1) * ST_W]

        def ssm_step(t, carry, ar=ar, ai=ai, sb=sb):
            xr, xi = carry
            row = _slab(t)
            bur = sb[pl.ds(row, BATCH), 0:ST_W]
            bui = sb[pl.ds(row, BATCH), ST_W:2 * ST_W]
            nxr = ar * xr - ai * xi + bur
            nxi = ar * xi + ai * xr + bui
            sb[pl.ds(row, BATCH), 0:ST_W] = nxr
            sb[pl.ds(row, BATCH), ST_W:2 * ST_W] = nxi
            return nxr, nxi

        xr, xi = _unrolled_loop(TL, SCAN_UNROLL, ssm_step,
                                (s_re[:, c * ST_W:(c + 1) * ST_W], s_im[:, c * ST_W:(c + 1) * ST_W]))
        s_re[:, c * ST_W:(c + 1) * ST_W] = xr
        s_im[:, c * ST_W:(c + 1) * ST_W] = xi
        ys.append(jnp.dot(sb[...].astype(bf16), cmat_ref[c], preferred_element_type=f32))
    y = jnp.concatenate(ys, axis=1) + dskip_ref[...] * ssm_br
    y = _gelu(y)
    zz = jnp.dot(y.astype(bf16), w_glu_ref[...], preferred_element_type=f32) + b_glu_ref[...]
    y_ssm = zz[:, :D_SSM] * _sigmoid(zz[:, D_SSM:])
    y_ssm = y_ssm * lax.rsqrt(jnp.mean(y_ssm * y_ssm, axis=-1, keepdims=True) + RMS_EPS) * g_ssm_ref[...]
    ycat[:, D_REC:] = y_ssm.astype(bf16)

    mix = jnp.dot(ycat[...], w_out_ref[...], preferred_element_type=f32)
    v = ALPHA * x + mix
    mu = jnp.mean(v, axis=-1, keepdims=True)
    vc = v - mu
    var = jnp.mean(vc * vc, axis=-1, keepdims=True)
    x1 = vc * lax.rsqrt(var + LN_EPS) * ln_g_ref[...] + ln_b_ref[...]
    x1_ref[...] = x1

    x_hi = x1.astype(bf16)
    x_lo = (x1 - x_hi.astype(f32)).astype(bf16)
    hh = jnp.dot(x_hi, wr_ref[...], preferred_element_type=f32)
    lh = jnp.dot(x_lo, wr_ref[:, 0:ROUTER_W], preferred_element_type=f32)
    logits = hh[:, :ROUTER_W] + hh[:, ROUTER_W:] + lh + rb_ref[...]
    lt = logits.T

    gl = [lt[j:j + 1, :] for j in range(N_EXPERT_GROUPS)]
    gmax = jnp.maximum(jnp.maximum(gl[0], gl[1]), jnp.maximum(gl[2], gl[3]))
    gidx = jnp.where(gl[0] == gmax, 0, jnp.where(gl[1] == gmax, 1, jnp.where(gl[2] == gmax, 2, 3))).astype(i32)
    gsum = (jnp.exp(gl[0] - gmax) + jnp.exp(gl[1] - gmax)) + (jnp.exp(gl[2] - gmax) + jnp.exp(gl[3] - gmax))
    g_top = 1.0 / gsum
    ets = [lt[E_LOGIT_OFF + EXPERTS_PER_GROUP * g:E_LOGIT_OFF + EXPERTS_PER_GROUP * (g + 1), :]
           for g in range(N_EXPERT_GROUPS)]
    e_in = jnp.where(gidx == 0, ets[0], jnp.where(gidx == 1, ets[1], jnp.where(gidx == 2, ets[2], ets[3])))
    sub = lax.broadcasted_iota(i32, (EXPERTS_PER_GROUP, R), 0)
    m1 = jnp.max(e_in, axis=0, keepdims=True)
    i1 = jnp.min(jnp.where(e_in == m1, sub, EXPERTS_PER_GROUP), axis=0, keepdims=True)
    rest = jnp.where(sub == i1, -jnp.inf, e_in)
    m2 = jnp.max(rest, axis=0, keepdims=True)
    i2 = jnp.min(jnp.where(rest == m2, sub, EXPERTS_PER_GROUP), axis=0, keepdims=True)
    p2 = jnp.exp(m2 - m1)
    inv = g_top / (1.0 + p2)
    e1 = gidx * EXPERTS_PER_GROUP + i1
    e2 = gidx * EXPERTS_PER_GROUP + i2

    sub32 = lax.broadcasted_iota(i32, (N_EXPERTS, R), 0)
    oh1 = (sub32 == e1).astype(f32)
    oh2 = (sub32 == e2).astype(f32)
    both = oh1 + oh2
    prefix = jnp.dot(both.astype(bf16), tri_ref[...], preferred_element_type=f32)
    base = prefix + cnt_carry[:, 0:1]
    rank1 = jnp.sum(oh1 * base, axis=0, keepdims=True)
    rank2 = jnp.sum(oh2 * base, axis=0, keepdims=True)
    cnt_carry[...] = cnt_carry[...] + jnp.sum(both, axis=1, keepdims=True)
    cnt_ref[...] = cnt_carry[...]

    eid_ref[...] = jnp.zeros_like(eid_ref)
    eid_ref[0:1, :] = e1
    eid_ref[1:2, :] = e2
    rank_ref[...] = jnp.zeros_like(rank_ref)
    rank_ref[0:1, :] = rank1.astype(i32)
    rank_ref[1:2, :] = rank2.astype(i32)
    gate_ref[...] = jnp.zeros_like(gate_ref)
    gate_ref[0:1, :] = inv
    gate_ref[1:2, :] = inv * p2

    if fused_in:
        @pl.when(step == pl.num_programs(0) - 1)
        def _():
            _wait_row_gather(yb_hbm, ybuf, sems, 1 - slot)


def _mixer_call(lw, x=None, fused=None):
    R = R_MIX
    halo = (CONV_WIDTH - 1) * BATCH
    n_steps = SEQ // TL
    weights = [lw["w_in"], lw["conv_w"], lw["conv_b"], lw["w_gate"], lw["ba"], lw["bx"], lw["lam"],
               lw["abar_re"], lw["abar_im"], lw["bbar"], lw["cmat"], lw["dskip"], lw["w_glu"], lw["b_glu"],
               lw["g_rec"], lw["g_ssm"], lw["w_out"], lw["ln1_g"], lw["ln1_b"], lw["wr"], lw["rb"], lw["tri"]]
    scratch = [
        pltpu.VMEM((R + halo, D_REC), f32),
        pltpu.VMEM((R, D_REC), f32),
        pltpu.VMEM((R, D_REC), f32),
        pltpu.VMEM((BATCH, D_REC), f32),
        pltpu.VMEM((SSM_CHUNKS, R, 2 * ST_W), f32),
        pltpu.VMEM((BATCH, SSM_GROUPS * SSM_STATE), f32),
        pltpu.VMEM((BATCH, SSM_GROUPS * SSM_STATE), f32),
        pltpu.VMEM((N_EXPERTS, LANES), f32),
        pltpu.VMEM((R, D_MODEL), bf16),
    ]
    if fused is None:
        inputs = [x]
        in_specs = [pl.BlockSpec((BATCH, TL, D_MODEL), lambda i: (0, i, 0))]
    else:
        dest_tiles, x1_prev, gates_t, yb, ln2_g, ln2_b = fused
        inputs = [dest_tiles, dest_tiles, x1_prev, gates_t, yb, ln2_g, ln2_b]
        in_specs = [pl.BlockSpec((1, 1, 2 * TC), lambda i: (i, 0, 0), memory_space=pltpu.SMEM),
                    pl.BlockSpec((1, 1, 2 * TC), lambda i: (jnp.minimum(i + 1, n_steps - 1), 0, 0),
                                 memory_space=pltpu.SMEM),
                    pl.BlockSpec((R, D_MODEL), lambda i: (i, 0)),
                    pl.BlockSpec((R, 2), lambda i: (i, 0)),
                    pl.BlockSpec(memory_space=pl.ANY),
                    _full((1, D_MODEL)), _full((1, D_MODEL))]
        scratch = scratch + [pltpu.VMEM((2, 2, TC, D_MODEL), f32), pltpu.SemaphoreType.DMA((2,))]
    in_specs = in_specs + [_full(w.shape) for w in weights]
    out_shape = (
        jax.ShapeDtypeStruct((T_TOK, D_MODEL), f32),
        jax.ShapeDtypeStruct((SUBLANES, T_TOK), i32),
        jax.ShapeDtypeStruct((SUBLANES, T_TOK), f32),
        jax.ShapeDtypeStruct((SUBLANES, T_TOK), i32),
        jax.ShapeDtypeStruct((N_EXPERTS, LANES), f32),
    )
    out_specs = (
        pl.BlockSpec((R, D_MODEL), lambda i: (i, 0)),
        pl.BlockSpec((SUBLANES, R), lambda i: (0, i)),
        pl.BlockSpec((SUBLANES, R), lambda i: (0, i)),
        pl.BlockSpec((SUBLANES, R), lambda i: (0, i)),
        pl.BlockSpec((N_EXPERTS, LANES), lambda i: (0, 0)),
    )
    return pl.pallas_call(
        functools.partial(_mixer_kernel, fused is not None), grid=(n_steps,), in_specs=in_specs,
        out_specs=out_specs, out_shape=out_shape, scratch_shapes=scratch, name="mixer",
        compiler_params=pltpu.CompilerParams(dimension_semantics=("arbitrary",), vmem_limit_bytes=VMEM_LIMIT),
    )(*inputs, *weights)


def _dispatch_kernel(tail_ref, dest_ref, x_ref, xb_hbm, zbuf, sem):
    @pl.when(pl.program_id(0) == 0)
    def _():
        zbuf[...] = jnp.zeros_like(zbuf)

        def fill_block(blk):
            row0 = pl.multiple_of(blk * MOE_BLK, MOE_BLK)
            fill = pltpu.make_async_copy(zbuf, xb_hbm.at[pl.ds(row0, MOE_BLK)], sem)
            fill.start()
            fill.wait()

        for e in range(N_EXPERTS):
            fill_block(tail_ref[e])

        def fill_spare(blk, c):
            fill_block(blk)
            return c

        lax.fori_loop(tail_ref[N_EXPERTS], N_BLOCKS, fill_spare, 0)

    def issue(j, c):
        for k in range(2):
            pltpu.make_async_copy(x_ref.at[pl.ds(j, 1)], xb_hbm.at[pl.ds(dest_ref[0, 0, k * TD + j], 1)], sem).start()
        return c

    _unrolled_loop(TD, DMA_UNROLL, issue, 0)
    for k in range(2):
        pltpu.make_async_copy(x_ref, xb_hbm.at[pl.ds(0, TD)], sem).wait()


def _dispatch_call(tail_blocks, dest_tiles, x1):
    grid_spec = pltpu.PrefetchScalarGridSpec(
        num_scalar_prefetch=1, grid=(T_TOK // TD,),
        in_specs=[pl.BlockSpec((1, 1, 2 * TD), lambda i, tail: (i, 0, 0), memory_space=pltpu.SMEM),
                  pl.BlockSpec((TD, D_MODEL), lambda i, tail: (i, 0))],
        out_specs=pl.BlockSpec(memory_space=pl.ANY),
        scratch_shapes=[pltpu.VMEM((MOE_BLK, D_MODEL), f32), pltpu.SemaphoreType.DMA(())])
    return pl.pallas_call(
        _dispatch_kernel, grid_spec=grid_spec,
        out_shape=jax.ShapeDtypeStruct((P_ROWS, D_MODEL), f32), name="dispatch",
        compiler_params=pltpu.CompilerParams(dimension_semantics=("arbitrary",)),
    )(tail_blocks, dest_tiles, x1)


def _expert_kernel(be_ref, nh_ref, nb_ref, xb_ref, wg_ref, wu_ref, wd_ref, yb_ref, wg_s, wu_s, wd_s):
    del nb_ref
    i = pl.program_id(0)
    prev = be_ref[jnp.maximum(i - 1, 0)]

    @pl.when((i == 0) | (be_ref[i] != prev))
    def _():
        wg_s[...] = wg_ref[0].astype(bf16)
        wu_s[...] = wu_ref[0].astype(bf16)
        wd_s[...] = wd_ref[0].astype(bf16)

    for hb in range(MOE_BLK // MOE_SUB):
        rows = pl.ds(hb * MOE_SUB, MOE_SUB)

        @pl.when(hb < nh_ref[i])
        def _():
            xb = xb_ref[rows, :].astype(bf16)
            g = jnp.dot(xb, wg_s[...], preferred_element_type=f32)
            u = jnp.dot(xb, wu_s[...], preferred_element_type=f32)
            h = (g * _sigmoid(g)) * u
            yb_ref[rows, :] = jnp.dot(h.astype(bf16), wd_s[...], preferred_element_type=f32)

        @pl.when(hb >= nh_ref[i])
        def _():
            yb_ref[rows, :] = jnp.zeros((MOE_SUB, D_MODEL), f32)


def _expert_call(block_e, n_half, n_used, xb, w_gate, w_up, w_down):
    def row_block(i, be, nh, nb):
        return (jnp.minimum(i, nb[0] - 1), 0)

    grid_spec = pltpu.PrefetchScalarGridSpec(
        num_scalar_prefetch=3, grid=(N_BLOCKS,),
        in_specs=[pl.BlockSpec((MOE_BLK, D_MODEL), row_block),
                  pl.BlockSpec((1, D_MODEL, D_EXPERT), lambda i, be, nh, nb: (be[i], 0, 0)),
                  pl.BlockSpec((1, D_MODEL, D_EXPERT), lambda i, be, nh, nb: (be[i], 0, 0)),
                  pl.BlockSpec((1, D_EXPERT, D_MODEL), lambda i, be, nh, nb: (be[i], 0, 0))],
        out_specs=pl.BlockSpec((MOE_BLK, D_MODEL), lambda i, be, nh, nb: (i, 0)),
        scratch_shapes=[pltpu.VMEM((D_MODEL, D_EXPERT), bf16),
                        pltpu.VMEM((D_MODEL, D_EXPERT), bf16),
                        pltpu.VMEM((D_EXPERT, D_MODEL), bf16)])
    return pl.pallas_call(
        _expert_kernel, grid_spec=grid_spec,
        out_shape=jax.ShapeDtypeStruct((P_ROWS, D_MODEL), f32), name="experts",
        compiler_params=pltpu.CompilerParams(dimension_semantics=("arbitrary",), vmem_limit_bytes=VMEM_LIMIT),
    )(block_e, n_half, n_used, xb, w_gate, w_up, w_down)


def _combine_kernel(dest_ref, dest_next_ref, x1_ref, g_ref, yb_hbm, ln_g_ref, ln_b_ref, out_ref, ybuf, sems):
    i = pl.program_id(0)
    slot = i % 2

    @pl.when(i == 0)
    def _():
        _issue_row_gather(dest_ref, yb_hbm, ybuf, sems, 0, DMA_UNROLL)

    @pl.when(i + 1 < pl.num_programs(0))
    def _():
        _issue_row_gather(dest_next_ref, yb_hbm, ybuf, sems, 1 - slot, DMA_UNROLL)

    _wait_row_gather(yb_hbm, ybuf, sems, slot)
    res = _combine_rows(x1_ref[...], g_ref[...], ybuf[slot, 0], ybuf[slot, 1], ln_g_ref[...], ln_b_ref[...])
    out_ref[...] = jnp.swapaxes(res.reshape(TC // BATCH, BATCH, D_MODEL), 0, 1)


def _combine_call(dest_tiles, x1, gates_t, yb, ln_g, ln_b):
    n_steps = T_TOK // TC
    return pl.pallas_call(
        _combine_kernel, grid=(n_steps,),
        in_specs=[pl.BlockSpec((1, 1, 2 * TC), lambda i: (i, 0, 0), memory_space=pltpu.SMEM),
                  pl.BlockSpec((1, 1, 2 * TC), lambda i: (jnp.minimum(i + 1, n_steps - 1), 0, 0),
                               memory_space=pltpu.SMEM),
                  pl.BlockSpec((TC, D_MODEL), lambda i: (i, 0)),
                  pl.BlockSpec((TC, 2), lambda i: (i, 0)),
                  pl.BlockSpec(memory_space=pl.ANY),
                  _full((1, D_MODEL)), _full((1, D_MODEL))],
        out_specs=pl.BlockSpec((BATCH, TC // BATCH, D_MODEL), lambda i: (0, i, 0)),
        out_shape=jax.ShapeDtypeStruct((BATCH, SEQ, D_MODEL), f32),
        scratch_shapes=[pltpu.VMEM((2, 2, TC, D_MODEL), f32), pltpu.SemaphoreType.DMA((2,))],
        name="combine",
        compiler_params=pltpu.CompilerParams(dimension_semantics=("arbitrary",), vmem_limit_bytes=VMEM_LIMIT),
    )(dest_tiles, dest_tiles, x1, gates_t, yb, ln_g, ln_b)


def _slot_kernel(eid_ref, rank_ref, start_ref, dest_ref):
    sub = lax.broadcasted_iota(i32, (N_EXPERTS, TS), 0)
    starts = start_ref[...]
    dest_ref[...] = rank_ref[...]
    for k in range(2):
        e = eid_ref[k:k + 1, :]
        start = jnp.sum(jnp.where(sub == e, starts, 0), axis=0, keepdims=True)
        dest_ref[k:k + 1, :] = start + rank_ref[k:k + 1, :]


def _slot_call(eid, rank, pad_starts):
    return pl.pallas_call(
        _slot_kernel, grid=(T_TOK // TS,),
        in_specs=[pl.BlockSpec((SUBLANES, TS), lambda i: (0, i)),
                  pl.BlockSpec((SUBLANES, TS), lambda i: (0, i)),
                  _full((N_EXPERTS, 1))],
        out_specs=pl.BlockSpec((SUBLANES, TS), lambda i: (0, i)),
        out_shape=jax.ShapeDtypeStruct((SUBLANES, T_TOK), i32), name="slots",
    )(eid, rank, pad_starts.reshape(N_EXPERTS, 1))


def _tile_dest(dest, tile):
    return jnp.transpose(dest.reshape(2, T_TOK // tile, tile), (1, 0, 2)).reshape(T_TOK // tile, 1, 2 * tile)


def _block_diag(blocks):
    n, a, b = blocks.shape
    eye = jnp.eye(n, dtype=blocks.dtype)
    return jnp.einsum("nab,nm->namb", blocks, eye).reshape(n * a, n * b)


def kernel(x, w_in, conv_w, conv_b, lru_wa, lru_ba, lru_wx, lru_bx, lru_lambda, ssm_lambda_re, ssm_lambda_im, ssm_log_dt, ssm_b_re, ssm_b_im, ssm_c_re, ssm_c_im, ssm_d, w_glu, b_glu, g_rec, g_ssm, w_out, ln1_g, ln1_b, router_wg, router_bg, router_we, router_be, exp_w_gate, exp_w_up, exp_w_down, ln2_g, ln2_b):
    a_re, a_im, bb_re, bb_im = _ssm_prep(ssm_lambda_re, ssm_lambda_im, ssm_log_dt, ssm_b_re, ssm_b_im)
    n_state = SSM_GROUPS * SSM_STATE
    a_re = a_re.reshape(DEPTH, SSM_GROUPS, SSM_GROUP, SSM_STATE)[:, :, 0, :].reshape(DEPTH, 1, n_state)
    a_im = a_im.reshape(DEPTH, SSM_GROUPS, SSM_GROUP, SSM_STATE)[:, :, 0, :].reshape(DEPTH, 1, n_state)
    a_re = jnp.broadcast_to(a_re, (DEPTH, BATCH, n_state))
    a_im = jnp.broadcast_to(a_im, (DEPTH, BATCH, n_state))
    gpc = SSM_GROUPS // SSM_CHUNKS
    tri = (lax.broadcasted_iota(i32, (R_MIX, R_MIX), 0) < lax.broadcasted_iota(i32, (R_MIX, R_MIX), 1)).astype(bf16)
    half = D_REC // 2

    layers = []
    for l in range(DEPTH):
        bre = bb_re.reshape(DEPTH, SSM_CHUNKS, gpc, SSM_GROUP, SSM_STATE)[l]
        bim = bb_im.reshape(DEPTH, SSM_CHUNKS, gpc, SSM_GROUP, SSM_STATE)[l]
        bbar = jnp.stack([jnp.concatenate([_block_diag(bre[c]), _block_diag(bim[c])], axis=1)
                          for c in range(SSM_CHUNKS)]).astype(bf16)
        cre = jnp.transpose(ssm_c_re[l].reshape(SSM_CHUNKS, gpc, SSM_GROUP, SSM_STATE), (0, 1, 3, 2))
        cim = jnp.transpose(ssm_c_im[l].reshape(SSM_CHUNKS, gpc, SSM_GROUP, SSM_STATE), (0, 1, 3, 2))
        cmat = jnp.stack([jnp.concatenate([_block_diag(cre[c]), -_block_diag(cim[c])], axis=0)
                          for c in range(SSM_CHUNKS)]).astype(bf16)
        wa_d = _block_diag(lru_wa[l])
        wx_d = _block_diag(lru_wx[l])
        w_gate = jnp.stack([jnp.concatenate([wa_d[h * half:(h + 1) * half, h * half:(h + 1) * half],
                                             wx_d[h * half:(h + 1) * half, h * half:(h + 1) * half]], axis=1)
                            for h in range(2)]).astype(bf16)
        wr32 = jnp.zeros((D_MODEL, ROUTER_W), f32)
        wr32 = wr32.at[:, 0:N_EXPERT_GROUPS].set(router_wg[l])
        wr32 = wr32.at[:, E_LOGIT_OFF:E_LOGIT_OFF + N_EXPERTS].set(router_we[l])
        wr_hi = wr32.astype(bf16)
        wr_lo = (wr32 - wr_hi.astype(f32)).astype(bf16)
        rb = jnp.zeros((1, ROUTER_W), f32)
        rb = rb.at[0, 0:N_EXPERT_GROUPS].set(router_bg[l])
        rb = rb.at[0, E_LOGIT_OFF:E_LOGIT_OFF + N_EXPERTS].set(router_be[l])
        layers.append(dict(
            w_in=w_in[l].astype(bf16), conv_w=conv_w[l], conv_b=conv_b[l].reshape(1, D_REC),
            w_gate=w_gate, ba=lru_ba[l].reshape(1, D_REC), bx=lru_bx[l].reshape(1, D_REC),
            lam=lru_lambda[l].reshape(1, D_REC), abar_re=a_re[l], abar_im=a_im[l], bbar=bbar, cmat=cmat,
            dskip=ssm_d[l].reshape(1, D_SSM), w_glu=w_glu[l].astype(bf16), b_glu=b_glu[l].reshape(1, 2 * D_SSM),
            g_rec=g_rec[l].reshape(1, D_REC), g_ssm=g_ssm[l].reshape(1, D_SSM), w_out=w_out[l].astype(bf16),
            ln1_g=ln1_g[l].reshape(1, D_MODEL), ln1_b=ln1_b[l].reshape(1, D_MODEL),
            wr=jnp.concatenate([wr_hi, wr_lo], axis=1), rb=rb, tri=tri))

    wg_all = exp_w_gate.reshape(DEPTH * N_EXPERTS, D_MODEL, D_EXPERT)
    wu_all = exp_w_up.reshape(DEPTH * N_EXPERTS, D_MODEL, D_EXPERT)
    wd_all = exp_w_down.reshape(DEPTH * N_EXPERTS, D_EXPERT, D_MODEL)
    fused = None
    for l in range(DEPTH):
        x1, eid, gates, rank, cnt = _mixer_call(layers[l], x=x if l == 0 else None, fused=fused)
        counts = cnt[:, 0].astype(i32)
        n_blk = (counts + MOE_BLK - 1) // MOE_BLK
        blk_ends = jnp.cumsum(n_blk)
        blk_starts = blk_ends - n_blk
        dest = _slot_call(eid, rank, blk_starts * MOE_BLK)[0:2]
        n_used = blk_ends[-1:].astype(i32)
        blk = jnp.arange(N_BLOCKS, dtype=i32)
        block_e = jnp.minimum(jnp.sum((blk_ends[None, :] <= blk[:, None]).astype(i32), axis=1), N_EXPERTS - 1)
        rows_in_blk = jnp.clip(counts[block_e] - (blk - blk_starts[block_e]) * MOE_BLK, 0, MOE_BLK)
        n_half = jnp.where(blk < n_used[0], (rows_in_blk + MOE_SUB - 1) // MOE_SUB, 0).astype(i32)
        tail_blocks = jnp.concatenate([jnp.where(n_blk > 0, blk_ends - 1, N_BLOCKS - 1).astype(i32), n_used])
        xb = _dispatch_call(tail_blocks, _tile_dest(dest, TD), x1)
        yb = _expert_call(block_e + l * N_EXPERTS, n_half, n_used, xb, wg_all, wu_all, wd_all)
        fused = (_tile_dest(dest, TC), x1, jnp.transpose(gates[0:2]), yb,
                 ln2_g[l].reshape(1, D_MODEL), ln2_b[l].reshape(1, D_MODEL))
    return _combine_call(*fused)
```

```python
import functools
import math

import jax
import jax.numpy as jnp
from jax import lax
from jax.experimental import pallas as pl
from jax.experimental.pallas import tpu as pltpu

D_MODEL = 1024
BATCH = 8
SEQ = 4096
DEPTH = 4
D_REC = 512
D_SSM = 512
REC_HEADS = 8
REC_HEAD_DIM = 64
CONV_WIDTH = 4
LRU_C = 8.0
SSM_GROUP = 16
SSM_GROUPS = 32
SSM_STATE = 64
N_EXPERT_GROUPS = 4
EXPERTS_PER_GROUP = 8
N_EXPERTS = 32
D_EXPERT = 512
ALPHA = (2.0 * DEPTH) ** 0.25
LN_EPS = 1e-5
RMS_EPS = 1e-6

T_TOK = BATCH * SEQ

SUBLANES = 8
LANES = 128
TL = 64
R_MIX = TL * BATCH
SCAN_UNROLL = TL
SSM_CHUNKS = 4
CH_W = D_SSM // SSM_CHUNKS
ST_W = SSM_GROUPS * SSM_STATE // SSM_CHUNKS
ROUTER_W = 128
E_LOGIT_OFF = 8
MOE_BLK = 256
N_BLOCKS = (T_TOK * 2) // MOE_BLK + N_EXPERTS
P_ROWS = N_BLOCKS * MOE_BLK
TD = 4096
TC = R_MIX
TS = 4096
DMA_UNROLL = 8
VMEM_LIMIT = 56 * 1024 * 1024

f32 = jnp.float32
bf16 = jnp.bfloat16
i32 = jnp.int32


def _full(shape):
    n = len(shape)
    return pl.BlockSpec(shape, lambda *_: (0,) * n)


def _sigmoid(v):
    return 1.0 / (1.0 + jnp.exp(-v))


def _gelu(v):
    return 0.5 * v * (1.0 + jnp.tanh(0.7978845608028654 * (v + 0.044715 * (v * v * v))))


def _slab(t):
    return t * BATCH if isinstance(t, int) else pl.multiple_of(t * BATCH, BATCH)


def _unrolled_loop(n_steps, unroll, body, init):
    if unroll >= n_steps:
        carry = init
        for t in range(n_steps):
            carry = body(t, carry)
        return carry

    def outer(o, carry):
        for u in range(unroll):
            carry = body(o * unroll + u, carry)
        return carry
    return lax.fori_loop(0, n_steps // unroll, outer, init)


def _ssm_prep_kernel(lre_ref, lim_ref, ldt_ref, bre_ref, bim_ref,
                     are_ref, aim_ref, bbre_ref, bbim_ref):
    lre = jnp.minimum(lre_ref[...], -1e-4)
    lim = lim_ref[...]
    dt = jnp.exp(ldt_ref[...])
    mag = jnp.exp(lre * dt)
    a_re = mag * jnp.cos(lim * dt)
    a_im = mag * jnp.sin(lim * dt)
    den = lre * lre + lim * lim
    p_re = a_re - 1.0
    p_im = a_im
    coef_re = (p_re * lre + p_im * lim) / den
    coef_im = (p_im * lre - p_re * lim) / den
    br = bre_ref[...]
    bi = bim_ref[...]
    are_ref[...] = a_re
    aim_ref[...] = a_im
    bbre_ref[...] = coef_re * br - coef_im * bi
    bbim_ref[...] = coef_re * bi + coef_im * br


def _ssm_prep(lam_re, lam_im, log_dt, b_re, b_im):
    rows = DEPTH * SSM_GROUPS * SSM_GROUP

    def rep(a):
        return jnp.broadcast_to(a[:, :, None, :], (DEPTH, SSM_GROUPS, SSM_GROUP, SSM_STATE)).reshape(rows, SSM_STATE)

    ldt = jnp.broadcast_to(log_dt[:, :, None], (DEPTH, SSM_GROUPS, SSM_STATE))
    bre_t = jnp.transpose(b_re, (0, 1, 3, 2)).reshape(rows, SSM_STATE)
    bim_t = jnp.transpose(b_im, (0, 1, 3, 2)).reshape(rows, SSM_STATE)
    shp = jax.ShapeDtypeStruct((rows, SSM_STATE), f32)
    a_re, a_im, bb_re, bb_im = pl.pallas_call(
        _ssm_prep_kernel, out_shape=(shp, shp, shp, shp), name="ssm_prep",
    )(rep(lam_re), rep(lam_im), rep(ldt), bre_t, bim_t)
    return a_re, a_im, bb_re, bb_im


def _issue_row_gather(dref, yb_hbm, ybuf, sems, s, unroll):
    def issue(j, c):
        for k in range(2):
            pltpu.make_async_copy(yb_hbm.at[pl.ds(dref[0, 0, k * TC + j], 1)],
                                  ybuf.at[s, k, pl.ds(j, 1)], sems.at[s]).start()
        return c
    _unrolled_loop(TC, unroll, issue, 0)


def _wait_row_gather(yb_hbm, ybuf, sems, s):
    for k in range(2):
        pltpu.make_async_copy(yb_hbm.at[pl.ds(0, TC)], ybuf.at[s, k], sems.at[s]).wait()


def _combine_rows(x1, g, y0, y1, ln_g, ln_b):
    v = ALPHA * x1 + (g[:, 0:1] * y0 + g[:, 1:2] * y1)
    mu = jnp.mean(v, axis=-1, keepdims=True)
    vc = v - mu
    var = jnp.mean(vc * vc, axis=-1, keepdims=True)
    return vc * lax.rsqrt(var + LN_EPS) * ln_g + ln_b


def _mixer_kernel(fused_in, *refs):
    if fused_in:
        dest_ref, dest_next_ref, xprev_ref, g_ref, yb_hbm, ln2_g_ref, ln2_b_ref = refs[:7]
        refs = refs[7:]
        ybuf, sems = refs[-2:]
        refs = refs[:-2]
    else:
        x_ref = refs[0]
        refs = refs[1:]
    (w_in_ref, conv_w_ref, conv_b_ref, w_gate_ref, ba_ref, bx_ref, lam_ref,
     abar_re_ref, abar_im_ref, bbar_ref, cmat_ref, dskip_ref, w_glu_ref, b_glu_ref,
     g_rec_ref, g_ssm_ref, w_out_ref, ln_g_ref, ln_b_ref, wr_ref, rb_ref, tri_ref,
     x1_ref, eid_ref, gate_ref, rank_ref, cnt_ref,
     conv_buf, a_buf, b_buf, h_carry, s_buf, s_re, s_im, cnt_carry, ycat) = refs
    R = R_MIX
    halo = (CONV_WIDTH - 1) * BATCH
    step = pl.program_id(0)

    @pl.when(pl.program_id(0) == 0)
    def _():
        conv_buf[0:halo, :] = jnp.zeros((halo, D_REC), f32)
        h_carry[...] = jnp.zeros_like(h_carry)
        s_re[...] = jnp.zeros_like(s_re)
        s_im[...] = jnp.zeros_like(s_im)
        cnt_carry[...] = jnp.zeros_like(cnt_carry)

    if fused_in:
        slot = step % 2

        @pl.when(step == 0)
        def _():
            _issue_row_gather(dest_ref, yb_hbm, ybuf, sems, 0, DMA_UNROLL)

        _wait_row_gather(yb_hbm, ybuf, sems, slot)
        x = _combine_rows(xprev_ref[...], g_ref[...], ybuf[slot, 0], ybuf[slot, 1], ln2_g_ref[...], ln2_b_ref[...])
        _issue_row_gather(dest_next_ref, yb_hbm, ybuf, sems, 1 - slot, TC)
    else:
        x = jnp.swapaxes(x_ref[...], 0, 1).reshape(R, D_MODEL)
    xb = x.astype(bf16)
    gate_br = jnp.dot(xb, w_in_ref[:, 0:D_REC], preferred_element_type=f32)
    rec_br = jnp.dot(xb, w_in_ref[:, D_REC:2 * D_REC], preferred_element_type=f32)
    ssm_br = jnp.dot(xb, w_in_ref[:, 2 * D_REC:], preferred_element_type=f32)

    conv_buf[halo:halo + R, :] = rec_br
    cw = conv_w_ref[...]
    rec = conv_b_ref[...] + cw[3:4, :] * rec_br
    for k in range(CONV_WIDTH - 1):
        rec = rec + cw[k:k + 1, :] * conv_buf[k * BATCH:k * BATCH + R, :]
    conv_buf[0:halo, :] = conv_buf[R:R + halo, :]

    recb = rec.astype(bf16)
    half = D_REC // 2
    g0 = jnp.dot(recb[:, :half], w_gate_ref[0], preferred_element_type=f32)
    g1 = jnp.dot(recb[:, half:], w_gate_ref[1], preferred_element_type=f32)
    r_gate = _sigmoid(jnp.concatenate([g0[:, :half], g1[:, :half]], axis=1) + ba_ref[...])
    i_gate = _sigmoid(jnp.concatenate([g0[:, half:], g1[:, half:]], axis=1) + bx_ref[...])
    z = -lam_ref[...]
    softplus = jnp.maximum(z, 0.0) + jnp.log1p(jnp.exp(-jnp.abs(z)))
    log_a = (-LRU_C) * r_gate * softplus
    a = jnp.exp(log_a)
    mult = jnp.sqrt(-jnp.tanh(log_a) * (1.0 + a * a))
    a_buf[...] = a
    b_buf[...] = mult * (i_gate * rec)

    def lru_step(t, h):
        row = _slab(t)
        h = a_buf[pl.ds(row, BATCH), :] * h + b_buf[pl.ds(row, BATCH), :]
        b_buf[pl.ds(row, BATCH), :] = h
        return h

    h_carry[...] = _unrolled_loop(TL, SCAN_UNROLL, lru_step, h_carry[...])
    y_rec = _gelu(gate_br) * b_buf[...]
    y_rec = y_rec * lax.rsqrt(jnp.mean(y_rec * y_rec, axis=-1, keepdims=True) + RMS_EPS) * g_rec_ref[...]
    ycat[:, 0:D_REC] = y_rec.astype(bf16)

    ub = ssm_br.astype(bf16)
    ys = []
    for c in range(SSM_CHUNKS):
        sb = s_buf.at[c]
        sb[...] = jnp.dot(ub[:, c * CH_W:(c + 1) * CH_W], bbar_ref[c], preferred_element_type=f32)
        ar = abar_re_ref[:, c * ST_W:(c + 1) * ST_W]
        ai = abar_im_ref[:, c * ST_W:(c + 1) * ST_W]

        def ssm_step(t, carry, ar=ar, ai=ai, sb=sb):
            xr, xi = carry
            row = _slab(t)
            bur = sb[pl.ds(row, BATCH), 0:ST_W]
            bui = sb[pl.ds(row, BATCH), ST_W:2 * ST_W]
            nxr = ar * xr - ai * xi + bur
            nxi = ar * xi + ai * xr + bui
            sb[pl.ds(row, BATCH), 0:ST_W] = nxr
            sb[pl.ds(row, BATCH), ST_W:2 * ST_W] = nxi
            return nxr, nxi

        xr, xi = _unrolled_loop(TL, SCAN_UNROLL, ssm_step,
                                (s_re[:, c * ST_W:(c + 1) * ST_W], s_im[:, c * ST_W:(c + 1) * ST_W]))
        s_re[:, c * ST_W:(c + 1) * ST_W] = xr
        s_im[:, c * ST_W:(c + 1) * ST_W] = xi
        ys.append(jnp.dot(sb[...].astype(bf16), cmat_ref[c], preferred_element_type=f32))
    y = jnp.concatenate(ys, axis=1) + dskip_ref[...] * ssm_br
    y = _gelu(y)
    zz = jnp.dot(y.astype(bf16), w_glu_ref[...], preferred_element_type=f32) + b_glu_ref[...]
    y_ssm = zz[:, :D_SSM] * _sigmoid(zz[:, D_SSM:])
    y_ssm = y_ssm * lax.rsqrt(jnp.mean(y_ssm * y_ssm, axis=-1, keepdims=True) + RMS_EPS) * g_ssm_ref[...]
    ycat[:, D_REC:] = y_ssm.astype(bf16)

    mix = jnp.dot(ycat[...], w_out_ref[...], preferred_element_type=f32)
    v = ALPHA * x + mix
    mu = jnp.mean(v, axis=-1, keepdims=True)
    vc = v - mu
    var = jnp.mean(vc * vc, axis=-1, keepdims=True)
    x1 = vc * lax.rsqrt(var + LN_EPS) * ln_g_ref[...] + ln_b_ref[...]
    x1_ref[...] = x1

    x_hi = x1.astype(bf16)
    x_lo = (x1 - x_hi.astype(f32)).astype(bf16)
    hh = jnp.dot(x_hi, wr_ref[...], preferred_element_type=f32)
    lh = jnp.dot(x_lo, wr_ref[:, 0:ROUTER_W], preferred_element_type=f32)
    logits = hh[:, :ROUTER_W] + hh[:, ROUTER_W:] + lh + rb_ref[...]
    lt = logits.T

    gl = [lt[j:j + 1, :] for j in range(N_EXPERT_GROUPS)]
    gmax = jnp.maximum(jnp.maximum(gl[0], gl[1]), jnp.maximum(gl[2], gl[3]))
    gidx = jnp.where(gl[0] == gmax, 0, jnp.where(gl[1] == gmax, 1, jnp.where(gl[2] == gmax, 2, 3))).astype(i32)
    gsum = (jnp.exp(gl[0] - gmax) + jnp.exp(gl[1] - gmax)) + (jnp.exp(gl[2] - gmax) + jnp.exp(gl[3] - gmax))
    g_top = 1.0 / gsum
    ets = [lt[E_LOGIT_OFF + EXPERTS_PER_GROUP * g:E_LOGIT_OFF + EXPERTS_PER_GROUP * (g + 1), :]
           for g in range(N_EXPERT_GROUPS)]
    e_in = jnp.where(gidx == 0, ets[0], jnp.where(gidx == 1, ets[1], jnp.where(gidx == 2, ets[2], ets[3])))
    sub = lax.broadcasted_iota(i32, (EXPERTS_PER_GROUP, R), 0)
    m1 = jnp.max(e_in, axis=0, keepdims=True)
    i1 = jnp.min(jnp.where(e_in == m1, sub, EXPERTS_PER_GROUP), axis=0, keepdims=True)
    rest = jnp.where(sub == i1, -jnp.inf, e_in)
    m2 = jnp.max(rest, axis=0, keepdims=True)
    i2 = jnp.min(jnp.where(rest == m2, sub, EXPERTS_PER_GROUP), axis=0, keepdims=True)
    p2 = jnp.exp(m2 - m1)
    inv = g_top / (1.0 + p2)
    e1 = gidx * EXPERTS_PER_GROUP + i1
    e2 = gidx * EXPERTS_PER_GROUP + i2

    sub32 = lax.broadcasted_iota(i32, (N_EXPERTS, R), 0)
    oh1 = (sub32 == e1).astype(f32)
    oh2 = (sub32 == e2).astype(f32)
    both = oh1 + oh2
    prefix = jnp.dot(both.astype(bf16), tri_ref[...], preferred_element_type=f32)
    base = prefix + cnt_carry[:, 0:1]
    rank1 = jnp.sum(oh1 * base, axis=0, keepdims=True)
    rank2 = jnp.sum(oh2 * base, axis=0, keepdims=True)
    cnt_carry[...] = cnt_carry[...] + jnp.sum(both, axis=1, keepdims=True)
    cnt_ref[...] = cnt_carry[...]

    eid_ref[...] = jnp.zeros_like(eid_ref)
    eid_ref[0:1, :] = e1
    eid_ref[1:2, :] = e2
    rank_ref[...] = jnp.zeros_like(rank_ref)
    rank_ref[0:1, :] = rank1.astype(i32)
    rank_ref[1:2, :] = rank2.astype(i32)
    gate_ref[...] = jnp.zeros_like(gate_ref)
    gate_ref[0:1, :] = inv
    gate_ref[1:2, :] = inv * p2

    if fused_in:
        @pl.when(step == pl.num_programs(0) - 1)
        def _():
            _wait_row_gather(yb_hbm, ybuf, sems, 1 - slot)


def _mixer_call(lw, x=None, fused=None):
    R = R_MIX
    halo = (CONV_WIDTH - 1) * BATCH
    n_steps = SEQ // TL
    weights = [lw["w_in"], lw["conv_w"], lw["conv_b"], lw["w_gate"], lw["ba"], lw["bx"], lw["lam"],
               lw["abar_re"], lw["abar_im"], lw["bbar"], lw["cmat"], lw["dskip"], lw["w_glu"], lw["b_glu"],
               lw["g_rec"], lw["g_ssm"], lw["w_out"], lw["ln1_g"], lw["ln1_b"], lw["wr"], lw["rb"], lw["tri"]]
    scratch = [
        pltpu.VMEM((R + halo, D_REC), f32),
        pltpu.VMEM((R, D_REC), f32),
        pltpu.VMEM((R, D_REC), f32),
        pltpu.VMEM((BATCH, D_REC), f32),
        pltpu.VMEM((SSM_CHUNKS, R, 2 * ST_W), f32),
        pltpu.VMEM((BATCH, SSM_GROUPS * SSM_STATE), f32),
        pltpu.VMEM((BATCH, SSM_GROUPS * SSM_STATE), f32),
        pltpu.VMEM((N_EXPERTS, LANES), f32),
        pltpu.VMEM((R, D_MODEL), bf16),
    ]
    if fused is None:
        inputs = [x]
        in_specs = [pl.BlockSpec((BATCH, TL, D_MODEL), lambda i: (0, i, 0))]
    else:
        dest_tiles, x1_prev, gates_t, yb, ln2_g, ln2_b = fused
        inputs = [dest_tiles, dest_tiles, x1_prev, gates_t, yb, ln2_g, ln2_b]
        in_specs = [pl.BlockSpec((1, 1, 2 * TC), lambda i: (i, 0, 0), memory_space=pltpu.SMEM),
                    pl.BlockSpec((1, 1, 2 * TC), lambda i: (jnp.minimum(i + 1, n_steps - 1), 0, 0),
                                 memory_space=pltpu.SMEM),
                    pl.BlockSpec((R, D_MODEL), lambda i: (i, 0)),
                    pl.BlockSpec((R, 2), lambda i: (i, 0)),
                    pl.BlockSpec(memory_space=pl.ANY),
                    _full((1, D_MODEL)), _full((1, D_MODEL))]
        scratch = scratch + [pltpu.VMEM((2, 2, TC, D_MODEL), f32), pltpu.SemaphoreType.DMA((2,))]
    in_specs = in_specs + [_full(w.shape) for w in weights]
    out_shape = (
        jax.ShapeDtypeStruct((T_TOK, D_MODEL), f32),
        jax.ShapeDtypeStruct((SUBLANES, T_TOK), i32),
        jax.ShapeDtypeStruct((SUBLANES, T_TOK), f32),
        jax.ShapeDtypeStruct((SUBLANES, T_TOK), i32),
        jax.ShapeDtypeStruct((N_EXPERTS, LANES), f32),
    )
    out_specs = (
        pl.BlockSpec((R, D_MODEL), lambda i: (i, 0)),
        pl.BlockSpec((SUBLANES, R), lambda i: (0, i)),
        pl.BlockSpec((SUBLANES, R), lambda i: (0, i)),
        pl.BlockSpec((SUBLANES, R), lambda i: (0, i)),
        pl.BlockSpec((N_EXPERTS, LANES), lambda i: (0, 0)),
    )
    return pl.pallas_call(
        functools.partial(_mixer_kernel, fused is not None), grid=(n_steps,), in_specs=in_specs,
        out_specs=out_specs, out_shape=out_shape, scratch_shapes=scratch, name="mixer",
        compiler_params=pltpu.CompilerParams(dimension_semantics=("arbitrary",), vmem_limit_bytes=VMEM_LIMIT),
    )(*inputs, *weights)


def _slot_token_kernel(dest_ref, tok_ref):
    i = pl.program_id(0)

    @pl.when(i == 0)
    def _():
        def clear(o, c):
            for u in range(DMA_UNROLL):
                tok_ref[o * DMA_UNROLL + u] = 0
            return c
        lax.fori_loop(0, P_ROWS // DMA_UNROLL, clear, 0)

    def scatter(j, c):
        for k in range(2):
            tok_ref[dest_ref[0, 0, k * TD + j]] = i * TD + j
        return c

    _unrolled_loop(TD, DMA_UNROLL, scatter, 0)


def _slot_token_call(dest_tiles):
    return pl.pallas_call(
        _slot_token_kernel, grid=(T_TOK // TD,),
        in_specs=[pl.BlockSpec((1, 1, 2 * TD), lambda i: (i, 0, 0), memory_space=pltpu.SMEM)],
        out_specs=pl.BlockSpec((P_ROWS,), lambda i: (0,), memory_space=pltpu.SMEM),
        out_shape=jax.ShapeDtypeStruct((P_ROWS,), i32), name="slot_tokens",
        compiler_params=pltpu.CompilerParams(dimension_semantics=("arbitrary",)),
    )(dest_tiles)


def _expert_kernel(be_ref, nb_ref, tok_ref, tok_next_ref, x1_hbm, wg_ref, wu_ref, wd_ref, yb_ref,
                   xbuf, xbf, sems, wg_s, wu_s, wd_s):
    i = pl.program_id(0)
    n_used = nb_ref[0]
    slot = i % 2
    prev = be_ref[jnp.maximum(i - 1, 0)]

    def issue_gather(tref, s, unroll):
        def issue(j, c):
            pltpu.make_async_copy(x1_hbm.at[pl.ds(tref[0, 0, j], 1)], xbuf.at[s, pl.ds(j, 1)], sems.at[s]).start()
            return c
        _unrolled_loop(MOE_BLK, unroll, issue, 0)

    def wait_gather(s):
        pltpu.make_async_copy(x1_hbm.at[pl.ds(0, MOE_BLK)], xbuf.at[s], sems.at[s]).wait()

    @pl.when((i == 0) | (be_ref[i] != prev))
    def _():
        wg_s[...] = wg_ref[0].astype(bf16)
        wu_s[...] = wu_ref[0].astype(bf16)
        wd_s[...] = wd_ref[0].astype(bf16)

    @pl.when(i == 0)
    def _():
        issue_gather(tok_ref, 0, DMA_UNROLL)

    @pl.when(i < n_used)
    def _():
        wait_gather(slot)
        xbf[...] = xbuf[slot].astype(bf16)
        issue_gather(tok_next_ref, 1 - slot, MOE_BLK)
        xb = xbf[...]
        g = jnp.dot(xb, wg_s[...], preferred_element_type=f32)
        u = jnp.dot(xb, wu_s[...], preferred_element_type=f32)
        h = (g * _sigmoid(g)) * u
        yb_ref[...] = jnp.dot(h.astype(bf16), wd_s[...], preferred_element_type=f32)

        @pl.when(i == n_used - 1)
        def _():
            wait_gather(1 - slot)

    @pl.when(i >= n_used)
    def _():
        yb_ref[...] = jnp.zeros_like(yb_ref)


def _expert_call(block_e, n_used, slot_tok, x1, w_gate, w_up, w_down):
    tok_blocks = slot_tok.reshape(N_BLOCKS, 1, MOE_BLK)
    grid_spec = pltpu.PrefetchScalarGridSpec(
        num_scalar_prefetch=2, grid=(N_BLOCKS,),
        in_specs=[pl.BlockSpec((1, 1, MOE_BLK), lambda i, be, nb: (jnp.minimum(i, nb[0] - 1), 0, 0),
                               memory_space=pltpu.SMEM),
                  pl.BlockSpec((1, 1, MOE_BLK), lambda i, be, nb: (jnp.minimum(i + 1, nb[0] - 1), 0, 0),
                               memory_space=pltpu.SMEM),
                  pl.BlockSpec(memory_space=pl.ANY),
                  pl.BlockSpec((1, D_MODEL, D_EXPERT), lambda i, be, nb: (be[i], 0, 0)),
                  pl.BlockSpec((1, D_MODEL, D_EXPERT), lambda i, be, nb: (be[i], 0, 0)),
                  pl.BlockSpec((1, D_EXPERT, D_MODEL), lambda i, be, nb: (be[i], 0, 0))],
        out_specs=pl.BlockSpec((MOE_BLK, D_MODEL), lambda i, be, nb: (i, 0)),
        scratch_shapes=[pltpu.VMEM((2, MOE_BLK, D_MODEL), f32), pltpu.VMEM((MOE_BLK, D_MODEL), bf16),
                        pltpu.SemaphoreType.DMA((2,)),
                        pltpu.VMEM((D_MODEL, D_EXPERT), bf16),
                        pltpu.VMEM((D_MODEL, D_EXPERT), bf16),
                        pltpu.VMEM((D_EXPERT, D_MODEL), bf16)])
    return pl.pallas_call(
        _expert_kernel, grid_spec=grid_spec,
        out_shape=jax.ShapeDtypeStruct((P_ROWS, D_MODEL), f32), name="experts",
        compiler_params=pltpu.CompilerParams(dimension_semantics=("arbitrary",), vmem_limit_bytes=VMEM_LIMIT),
    )(block_e, n_used, tok_blocks, tok_blocks, x1, w_gate, w_up, w_down)


def _combine_kernel(dest_ref, dest_next_ref, x1_ref, g_ref, yb_hbm, ln_g_ref, ln_b_ref, out_ref, ybuf, sems):
    i = pl.program_id(0)
    slot = i % 2

    @pl.when(i == 0)
    def _():
        _issue_row_gather(dest_ref, yb_hbm, ybuf, sems, 0, DMA_UNROLL)

    @pl.when(i + 1 < pl.num_programs(0))
    def _():
        _issue_row_gather(dest_next_ref, yb_hbm, ybuf, sems, 1 - slot, DMA_UNROLL)

    _wait_row_gather(yb_hbm, ybuf, sems, slot)
    res = _combine_rows(x1_ref[...], g_ref[...], ybuf[slot, 0], ybuf[slot, 1], ln_g_ref[...], ln_b_ref[...])
    out_ref[...] = jnp.swapaxes(res.reshape(TC // BATCH, BATCH, D_MODEL), 0, 1)


def _combine_call(dest_tiles, x1, gates_t, yb, ln_g, ln_b):
    n_steps = T_TOK // TC
    return pl.pallas_call(
        _combine_kernel, grid=(n_steps,),
        in_specs=[pl.BlockSpec((1, 1, 2 * TC), lambda i: (i, 0, 0), memory_space=pltpu.SMEM),
                  pl.BlockSpec((1, 1, 2 * TC), lambda i: (jnp.minimum(i + 1, n_steps - 1), 0, 0),
                               memory_space=pltpu.SMEM),
                  pl.BlockSpec((TC, D_MODEL), lambda i: (i, 0)),
                  pl.BlockSpec((TC, 2), lambda i: (i, 0)),
                  pl.BlockSpec(memory_space=pl.ANY),
                  _full((1, D_MODEL)), _full((1, D_MODEL))],
        out_specs=pl.BlockSpec((BATCH, TC // BATCH, D_MODEL), lambda i: (0, i, 0)),
        out_shape=jax.ShapeDtypeStruct((BATCH, SEQ, D_MODEL), f32),
        scratch_shapes=[pltpu.VMEM((2, 2, TC, D_MODEL), f32), pltpu.SemaphoreType.DMA((2,))],
        name="combine",
        compiler_params=pltpu.CompilerParams(dimension_semantics=("arbitrary",), vmem_limit_bytes=VMEM_LIMIT),
    )(dest_tiles, dest_tiles, x1, gates_t, yb, ln_g, ln_b)


def _slot_kernel(eid_ref, rank_ref, start_ref, dest_ref):
    sub = lax.broadcasted_iota(i32, (N_EXPERTS, TS), 0)
    starts = start_ref[...]
    dest_ref[...] = rank_ref[...]
    for k in range(2):
        e = eid_ref[k:k + 1, :]
        start = jnp.sum(jnp.where(sub == e, starts, 0), axis=0, keepdims=True)
        dest_ref[k:k + 1, :] = start + rank_ref[k:k + 1, :]


def _slot_call(eid, rank, pad_starts):
    return pl.pallas_call(
        _slot_kernel, grid=(T_TOK // TS,),
        in_specs=[pl.BlockSpec((SUBLANES, TS), lambda i: (0, i)),
                  pl.BlockSpec((SUBLANES, TS), lambda i: (0, i)),
                  _full((N_EXPERTS, 1))],
        out_specs=pl.BlockSpec((SUBLANES, TS), lambda i: (0, i)),
        out_shape=jax.ShapeDtypeStruct((SUBLANES, T_TOK), i32), name="slots",
    )(eid, rank, pad_starts.reshape(N_EXPERTS, 1))


def _tile_dest(dest, tile):
    return jnp.transpose(dest.reshape(2, T_TOK // tile, tile), (1, 0, 2)).reshape(T_TOK // tile, 1, 2 * tile)


def _block_diag(blocks):
    n, a, b = blocks.shape
    eye = jnp.eye(n, dtype=blocks.dtype)
    return jnp.einsum("nab,nm->namb", blocks, eye).reshape(n * a, n * b)


def kernel(x, w_in, conv_w, conv_b, lru_wa, lru_ba, lru_wx, lru_bx, lru_lambda, ssm_lambda_re, ssm_lambda_im, ssm_log_dt, ssm_b_re, ssm_b_im, ssm_c_re, ssm_c_im, ssm_d, w_glu, b_glu, g_rec, g_ssm, w_out, ln1_g, ln1_b, router_wg, router_bg, router_we, router_be, exp_w_gate, exp_w_up, exp_w_down, ln2_g, ln2_b):
    a_re, a_im, bb_re, bb_im = _ssm_prep(ssm_lambda_re, ssm_lambda_im, ssm_log_dt, ssm_b_re, ssm_b_im)
    n_state = SSM_GROUPS * SSM_STATE
    a_re = a_re.reshape(DEPTH, SSM_GROUPS, SSM_GROUP, SSM_STATE)[:, :, 0, :].reshape(DEPTH, 1, n_state)
    a_im = a_im.reshape(DEPTH, SSM_GROUPS, SSM_GROUP, SSM_STATE)[:, :, 0, :].reshape(DEPTH, 1, n_state)
    a_re = jnp.broadcast_to(a_re, (DEPTH, BATCH, n_state))
    a_im = jnp.broadcast_to(a_im, (DEPTH, BATCH, n_state))
    gpc = SSM_GROUPS // SSM_CHUNKS
    tri = (lax.broadcasted_iota(i32, (R_MIX, R_MIX), 0) < lax.broadcasted_iota(i32, (R_MIX, R_MIX), 1)).astype(bf16)
    half = D_REC // 2

    layers = []
    for l in range(DEPTH):
        bre = bb_re.reshape(DEPTH, SSM_CHUNKS, gpc, SSM_GROUP, SSM_STATE)[l]
        bim = bb_im.reshape(DEPTH, SSM_CHUNKS, gpc, SSM_GROUP, SSM_STATE)[l]
        bbar = jnp.stack([jnp.concatenate([_block_diag(bre[c]), _block_diag(bim[c])], axis=1)
                          for c in range(SSM_CHUNKS)]).astype(bf16)
        cre = jnp.transpose(ssm_c_re[l].reshape(SSM_CHUNKS, gpc, SSM_GROUP, SSM_STATE), (0, 1, 3, 2))
        cim = jnp.transpose(ssm_c_im[l].reshape(SSM_CHUNKS, gpc, SSM_GROUP, SSM_STATE), (0, 1, 3, 2))
        cmat = jnp.stack([jnp.concatenate([_block_diag(cre[c]), -_block_diag(cim[c])], axis=0)
                          for c in range(SSM_CHUNKS)]).astype(bf16)
        wa_d = _block_diag(lru_wa[l])
        wx_d = _block_diag(lru_wx[l])
        w_gate = jnp.stack([jnp.concatenate([wa_d[h * half:(h + 1) * half, h * half:(h + 1) * half],
                                             wx_d[h * half:(h + 1) * half, h * half:(h + 1) * half]], axis=1)
                            for h in range(2)]).astype(bf16)
        wr32 = jnp.zeros((D_MODEL, ROUTER_W), f32)
        wr32 = wr32.at[:, 0:N_EXPERT_GROUPS].set(router_wg[l])
        wr32 = wr32.at[:, E_LOGIT_OFF:E_LOGIT_OFF + N_EXPERTS].set(router_we[l])
        wr_hi = wr32.astype(bf16)
        wr_lo = (wr32 - wr_hi.astype(f32)).astype(bf16)
        rb = jnp.zeros((1, ROUTER_W), f32)
        rb = rb.at[0, 0:N_EXPERT_GROUPS].set(router_bg[l])
        rb = rb.at[0, E_LOGIT_OFF:E_LOGIT_OFF + N_EXPERTS].set(router_be[l])
        layers.append(dict(
            w_in=w_in[l].astype(bf16), conv_w=conv_w[l], conv_b=conv_b[l].reshape(1, D_REC),
            w_gate=w_gate, ba=lru_ba[l].reshape(1, D_REC), bx=lru_bx[l].reshape(1, D_REC),
            lam=lru_lambda[l].reshape(1, D_REC), abar_re=a_re[l], abar_im=a_im[l], bbar=bbar, cmat=cmat,
            dskip=ssm_d[l].reshape(1, D_SSM), w_glu=w_glu[l].astype(bf16), b_glu=b_glu[l].reshape(1, 2 * D_SSM),
            g_rec=g_rec[l].reshape(1, D_REC), g_ssm=g_ssm[l].reshape(1, D_SSM), w_out=w_out[l].astype(bf16),
            ln1_g=ln1_g[l].reshape(1, D_MODEL), ln1_b=ln1_b[l].reshape(1, D_MODEL),
            wr=jnp.concatenate([wr_hi, wr_lo], axis=1), rb=rb, tri=tri))

    wg_all = exp_w_gate.reshape(DEPTH * N_EXPERTS, D_MODEL, D_EXPERT)
    wu_all = exp_w_up.reshape(DEPTH * N_EXPERTS, D_MODEL, D_EXPERT)
    wd_all = exp_w_down.reshape(DEPTH * N_EXPERTS, D_EXPERT, D_MODEL)
    fused = None
    for l in range(DEPTH):
        x1, eid, gates, rank, cnt = _mixer_call(layers[l], x=x if l == 0 else None, fused=fused)
        counts = cnt[:, 0].astype(i32)
        n_blk = (counts + MOE_BLK - 1) // MOE_BLK
        blk_ends = jnp.cumsum(n_blk)
        blk_starts = blk_ends - n_blk
        dest = _slot_call(eid, rank, blk_starts * MOE_BLK)[0:2]
        n_used = blk_ends[-1:].astype(i32)
        blk = jnp.arange(N_BLOCKS, dtype=i32)
        block_e = jnp.minimum(jnp.sum((blk_ends[None, :] <= blk[:, None]).astype(i32), axis=1), N_EXPERTS - 1)
        slot_tok = _slot_token_call(_tile_dest(dest, TD))
        yb = _expert_call(block_e + l * N_EXPERTS, n_used, slot_tok, x1, wg_all, wu_all, wd_all)
        fused = (_tile_dest(dest, TC), x1, jnp.transpose(gates[0:2]), yb,
                 ln2_g[l].reshape(1, D_MODEL), ln2_b[l].reshape(1, D_MODEL))
    return _combine_call(*fused)
```

```python
import functools
import math

import jax
import jax.numpy as jnp
from jax import lax
from jax.experimental import pallas as pl
from jax.experimental.pallas import tpu as pltpu

D_MODEL = 1024
BATCH = 8
SEQ = 4096
DEPTH = 4
D_REC = 512
D_SSM = 512
REC_HEADS = 8
REC_HEAD_DIM = 64
CONV_WIDTH = 4
LRU_C = 8.0
SSM_GROUP = 16
SSM_GROUPS = 32
SSM_STATE = 64
N_EXPERT_GROUPS = 4
EXPERTS_PER_GROUP = 8
N_EXPERTS = 32
D_EXPERT = 512
ALPHA = (2.0 * DEPTH) ** 0.25
LN_EPS = 1e-5
RMS_EPS = 1e-6

T_TOK = BATCH * SEQ

SUBLANES = 8
LANES = 128
TL = 64
R_MIX = TL * BATCH
SCAN_UNROLL = TL
SSM_CHUNKS = 4
CH_W = D_SSM // SSM_CHUNKS
ST_W = SSM_GROUPS * SSM_STATE // SSM_CHUNKS
ROUTER_W = 128
E_LOGIT_OFF = 8
MOE_BLK = 256
N_BLOCKS = (T_TOK * 2) // MOE_BLK + N_EXPERTS
P_ROWS = N_BLOCKS * MOE_BLK
TD = 2048
TC = R_MIX
TS = 4096
DMA_UNROLL = 8
VMEM_LIMIT = 56 * 1024 * 1024

f32 = jnp.float32
bf16 = jnp.bfloat16
i32 = jnp.int32


def _full(shape):
    n = len(shape)
    return pl.BlockSpec(shape, lambda *_: (0,) * n)


def _sigmoid(v):
    return 1.0 / (1.0 + jnp.exp(-v))


def _gelu(v):
    return 0.5 * v * (1.0 + jnp.tanh(0.7978845608028654 * (v + 0.044715 * (v * v * v))))


def _slab(t):
    return t * BATCH if isinstance(t, int) else pl.multiple_of(t * BATCH, BATCH)


def _unrolled_loop(n_steps, unroll, body, init):
    if unroll >= n_steps:
        carry = init
        for t in range(n_steps):
            carry = body(t, carry)
        return carry

    def outer(o, carry):
        for u in range(unroll):
            carry = body(o * unroll + u, carry)
        return carry
    return lax.fori_loop(0, n_steps // unroll, outer, init)


def _ssm_prep_kernel(lre_ref, lim_ref, ldt_ref, bre_ref, bim_ref,
                     are_ref, aim_ref, bbre_ref, bbim_ref):
    lre = jnp.minimum(lre_ref[...], -1e-4)
    lim = lim_ref[...]
    dt = jnp.exp(ldt_ref[...])
    mag = jnp.exp(lre * dt)
    a_re = mag * jnp.cos(lim * dt)
    a_im = mag * jnp.sin(lim * dt)
    den = lre * lre + lim * lim
    p_re = a_re - 1.0
    p_im = a_im
    coef_re = (p_re * lre + p_im * lim) / den
    coef_im = (p_im * lre - p_re * lim) / den
    br = bre_ref[...]
    bi = bim_ref[...]
    are_ref[...] = a_re
    aim_ref[...] = a_im
    bbre_ref[...] = coef_re * br - coef_im * bi
    bbim_ref[...] = coef_re * bi + coef_im * br


def _ssm_prep(lam_re, lam_im, log_dt, b_re, b_im):
    rows = DEPTH * SSM_GROUPS * SSM_GROUP

    def rep(a):
        return jnp.broadcast_to(a[:, :, None, :], (DEPTH, SSM_GROUPS, SSM_GROUP, SSM_STATE)).reshape(rows, SSM_STATE)

    ldt = jnp.broadcast_to(log_dt[:, :, None], (DEPTH, SSM_GROUPS, SSM_STATE))
    bre_t = jnp.transpose(b_re, (0, 1, 3, 2)).reshape(rows, SSM_STATE)
    bim_t = jnp.transpose(b_im, (0, 1, 3, 2)).reshape(rows, SSM_STATE)
    shp = jax.ShapeDtypeStruct((rows, SSM_STATE), f32)
    a_re, a_im, bb_re, bb_im = pl.pallas_call(
        _ssm_prep_kernel, out_shape=(shp, shp, shp, shp), name="ssm_prep",
    )(rep(lam_re), rep(lam_im), rep(ldt), bre_t, bim_t)
    return a_re, a_im, bb_re, bb_im


def _issue_row_gather(dref, yb_hbm, ybuf, sems, s, unroll):
    def issue(j, c):
        for k in range(2):
            pltpu.make_async_copy(yb_hbm.at[pl.ds(dref[0, 0, k * TC + j], 1)],
                                  ybuf.at[s, k, pl.ds(j, 1)], sems.at[s]).start(priority=k)
        return c
    _unrolled_loop(TC, unroll, issue, 0)


def _wait_row_gather(yb_hbm, ybuf, sems, s):
    for k in range(2):
        pltpu.make_async_copy(yb_hbm.at[pl.ds(0, TC)], ybuf.at[s, k], sems.at[s]).wait()


def _combine_rows(x1, g, y0, y1, ln_g, ln_b):
    v = ALPHA * x1 + (g[:, 0:1] * y0 + g[:, 1:2] * y1)
    mu = jnp.mean(v, axis=-1, keepdims=True)
    vc = v - mu
    var = jnp.mean(vc * vc, axis=-1, keepdims=True)
    return vc * lax.rsqrt(var + LN_EPS) * ln_g + ln_b


def _mixer_kernel(fused_in, *refs):
    if fused_in:
        dest_ref, dest_next_ref, xprev_ref, g_ref, yb_hbm, ln2_g_ref, ln2_b_ref = refs[:7]
        refs = refs[7:]
        ybuf, sems = refs[-2:]
        refs = refs[:-2]
    else:
        x_ref = refs[0]
        refs = refs[1:]
    (w_in_ref, conv_w_ref, conv_b_ref, w_gate_ref, ba_ref, bx_ref, lam_ref,
     abar_re_ref, abar_im_ref, bbar_ref, cmat_ref, dskip_ref, w_glu_ref, b_glu_ref,
     g_rec_ref, g_ssm_ref, w_out_ref, ln_g_ref, ln_b_ref, wr_ref, rb_ref, tri_ref,
     x1_ref, eid_ref, gate_ref, rank_ref, cnt_ref,
     conv_buf, a_buf, b_buf, h_carry, s_buf, s_re, s_im, cnt_carry, ycat) = refs
    R = R_MIX
    halo = (CONV_WIDTH - 1) * BATCH
    step = pl.program_id(0)

    @pl.when(pl.program_id(0) == 0)
    def _():
        conv_buf[0:halo, :] = jnp.zeros((halo, D_REC), f32)
        h_carry[...] = jnp.zeros_like(h_carry)
        s_re[...] = jnp.zeros_like(s_re)
        s_im[...] = jnp.zeros_like(s_im)
        cnt_carry[...] = jnp.zeros_like(cnt_carry)

    if fused_in:
        slot = step % 2

        @pl.when(step == 0)
        def _():
            _issue_row_gather(dest_ref, yb_hbm, ybuf, sems, 0, DMA_UNROLL)

        _wait_row_gather(yb_hbm, ybuf, sems, slot)
        x = _combine_rows(xprev_ref[...], g_ref[...], ybuf[slot, 0], ybuf[slot, 1], ln2_g_ref[...], ln2_b_ref[...])
        _issue_row_gather(dest_next_ref, yb_hbm, ybuf, sems, 1 - slot, TC)
    else:
        x = jnp.swapaxes(x_ref[...], 0, 1).reshape(R, D_MODEL)
    xb = x.astype(bf16)
    gate_br = jnp.dot(xb, w_in_ref[:, 0:D_REC], preferred_element_type=f32)
    rec_br = jnp.dot(xb, w_in_ref[:, D_REC:2 * D_REC], preferred_element_type=f32)
    ssm_br = jnp.dot(xb, w_in_ref[:, 2 * D_REC:], preferred_element_type=f32)

    conv_buf[halo:halo + R, :] = rec_br
    cw = conv_w_ref[...]
    rec = conv_b_ref[...] + cw[3:4, :] * rec_br
    for k in range(CONV_WIDTH - 1):
        rec = rec + cw[k:k + 1, :] * conv_buf[k * BATCH:k * BATCH + R, :]
    conv_buf[0:halo, :] = conv_buf[R:R + halo, :]

    recb = rec.astype(bf16)
    half = D_REC // 2
    g0 = jnp.dot(recb[:, :half], w_gate_ref[0], preferred_element_type=f32)
    g1 = jnp.dot(recb[:, half:], w_gate_ref[1], preferred_element_type=f32)
    r_gate = _sigmoid(jnp.concatenate([g0[:, :half], g1[:, :half]], axis=1) + ba_ref[...])
    i_gate = _sigmoid(jnp.concatenate([g0[:, half:], g1[:, half:]], axis=1) + bx_ref[...])
    z = -lam_ref[...]
    softplus = jnp.maximum(z, 0.0) + jnp.log1p(jnp.exp(-jnp.abs(z)))
    log_a = (-LRU_C) * r_gate * softplus
    a = jnp.exp(log_a)
    mult = jnp.sqrt(-jnp.tanh(log_a) * (1.0 + a * a))
    a_buf[...] = a
    b_buf[...] = mult * (i_gate * rec)

    def lru_step(t, h):
        row = _slab(t)
        h = a_buf[pl.ds(row, BATCH), :] * h + b_buf[pl.ds(row, BATCH), :]
        b_buf[pl.ds(row, BATCH), :] = h
        return h

    h_carry[...] = _unrolled_loop(TL, SCAN_UNROLL, lru_step, h_carry[...])
    y_rec = _gelu(gate_br) * b_buf[...]
    y_rec = y_rec * lax.rsqrt(jnp.mean(y_rec * y_rec, axis=-1, keepdims=True) + RMS_EPS) * g_rec_ref[...]
    ycat[:, 0:D_REC] = y_rec.astype(bf16)

    ub = ssm_br.astype(bf16)
    ys = []
    for c in range(SSM_CHUNKS):
        sb = s_buf.at[c]
        sb[...] = jnp.dot(ub[:, c * CH_W:(c + 1) * CH_W], bbar_ref[c], preferred_element_type=f32)
        ar = abar_re_ref[:, c * ST_W:(c + 1) * ST_W]
        ai = abar_im_ref[:, c * ST_W:(c + 1) * ST_W]

        def ssm_step(t, carry, ar=ar, ai=ai, sb=sb):
            xr, xi = carry
            row = _slab(t)
            bur = sb[pl.ds(row, BATCH), 0:ST_W]
            bui = sb[pl.ds(row, BATCH), ST_W:2 * ST_W]
            nxr = ar * xr - ai * xi + bur
            nxi = ar * xi + ai * xr + bui
            sb[pl.ds(row, BATCH), 0:ST_W] = nxr
            sb[pl.ds(row, BATCH), ST_W:2 * ST_W] = nxi
            return nxr, nxi

        xr, xi = _unrolled_loop(TL, SCAN_UNROLL, ssm_step,
                                (s_re[:, c * ST_W:(c + 1) * ST_W], s_im[:, c * ST_W:(c + 1) * ST_W]))
        s_re[:, c * ST_W:(c + 1) * ST_W] = xr
        s_im[:, c * ST_W:(c + 1) * ST_W] = xi
        ys.append(jnp.dot(sb[...].astype(bf16), cmat_ref[c], preferred_element_type=f32))
    y = jnp.concatenate(ys, axis=1) + dskip_ref[...] * ssm_br
    y = _gelu(y)
    zz = jnp.dot(y.astype(bf16), w_glu_ref[...], preferred_element_type=f32) + b_glu_ref[...]
    y_ssm = zz[:, :D_SSM] * _sigmoid(zz[:, D_SSM:])
    y_ssm = y_ssm * lax.rsqrt(jnp.mean(y_ssm * y_ssm, axis=-1, keepdims=True) + RMS_EPS) * g_ssm_ref[...]
    ycat[:, D_REC:] = y_ssm.astype(bf16)

    mix = jnp.dot(ycat[...], w_out_ref[...], preferred_element_type=f32)
    v = ALPHA * x + mix
    mu = jnp.mean(v, axis=-1, keepdims=True)
    vc = v - mu
    var = jnp.mean(vc * vc, axis=-1, keepdims=True)
    x1 = vc * lax.rsqrt(var + LN_EPS) * ln_g_ref[...] + ln_b_ref[...]
    x1_ref[...] = x1

    x_hi = x1.astype(bf16)
    x_lo = (x1 - x_hi.astype(f32)).astype(bf16)
    hh = jnp.dot(x_hi, wr_ref[...], preferred_element_type=f32)
    lh = jnp.dot(x_lo, wr_ref[:, 0:ROUTER_W], preferred_element_type=f32)
    logits = hh[:, :ROUTER_W] + hh[:, ROUTER_W:] + lh + rb_ref[...]
    lt = logits.T

    gl = [lt[j:j + 1, :] for j in range(N_EXPERT_GROUPS)]
    gmax = jnp.maximum(jnp.maximum(gl[0], gl[1]), jnp.maximum(gl[2], gl[3]))
    gidx = jnp.where(gl[0] == gmax, 0, jnp.where(gl[1] == gmax, 1, jnp.where(gl[2] == gmax, 2, 3))).astype(i32)
    gsum = (jnp.exp(gl[0] - gmax) + jnp.exp(gl[1] - gmax)) + (jnp.exp(gl[2] - gmax) + jnp.exp(gl[3] - gmax))
    g_top = 1.0 / gsum
    ets = [lt[E_LOGIT_OFF + EXPERTS_PER_GROUP * g:E_LOGIT_OFF + EXPERTS_PER_GROUP * (g + 1), :]
           for g in range(N_EXPERT_GROUPS)]
    e_in = jnp.where(gidx == 0, ets[0], jnp.where(gidx == 1, ets[1], jnp.where(gidx == 2, ets[2], ets[3])))
    sub = lax.broadcasted_iota(i32, (EXPERTS_PER_GROUP, R), 0)
    m1 = jnp.max(e_in, axis=0, keepdims=True)
    i1 = jnp.min(jnp.where(e_in == m1, sub, EXPERTS_PER_GROUP), axis=0, keepdims=True)
    rest = jnp.where(sub == i1, -jnp.inf, e_in)
    m2 = jnp.max(rest, axis=0, keepdims=True)
    i2 = jnp.min(jnp.where(rest == m2, sub, EXPERTS_PER_GROUP), axis=0, keepdims=True)
    p2 = jnp.exp(m2 - m1)
    inv = g_top / (1.0 + p2)
    e1 = gidx * EXPERTS_PER_GROUP + i1
    e2 = gidx * EXPERTS_PER_GROUP + i2

    sub32 = lax.broadcasted_iota(i32, (N_EXPERTS, R), 0)
    oh1 = (sub32 == e1).astype(f32)
    oh2 = (sub32 == e2).astype(f32)
    both = oh1 + oh2
    prefix = jnp.dot(both.astype(bf16), tri_ref[...], preferred_element_type=f32)
    base = prefix + cnt_carry[:, 0:1]
    rank1 = jnp.sum(oh1 * base, axis=0, keepdims=True)
    rank2 = jnp.sum(oh2 * base, axis=0, keepdims=True)
    cnt_carry[...] = cnt_carry[...] + jnp.sum(both, axis=1, keepdims=True)
    cnt_ref[...] = cnt_carry[...]

    eid_ref[...] = jnp.zeros_like(eid_ref)
    eid_ref[0:1, :] = e1
    eid_ref[1:2, :] = e2
    rank_ref[...] = jnp.zeros_like(rank_ref)
    rank_ref[0:1, :] = rank1.astype(i32)
    rank_ref[1:2, :] = rank2.astype(i32)
    gate_ref[...] = jnp.zeros_like(gate_ref)
    gate_ref[0:1, :] = inv
    gate_ref[1:2, :] = inv * p2

    if fused_in:
        @pl.when(step == pl.num_programs(0) - 1)
        def _():
            _wait_row_gather(yb_hbm, ybuf, sems, 1 - slot)


def _mixer_call(lw, x=None, fused=None):
    R = R_MIX
    halo = (CONV_WIDTH - 1) * BATCH
    n_steps = SEQ // TL
    weights = [lw["w_in"], lw["conv_w"], lw["conv_b"], lw["w_gate"], lw["ba"], lw["bx"], lw["lam"],
               lw["abar_re"], lw["abar_im"], lw["bbar"], lw["cmat"], lw["dskip"], lw["w_glu"], lw["b_glu"],
               lw["g_rec"], lw["g_ssm"], lw["w_out"], lw["ln1_g"], lw["ln1_b"], lw["wr"], lw["rb"], lw["tri"]]
    scratch = [
        pltpu.VMEM((R + halo, D_REC), f32),
        pltpu.VMEM((R, D_REC), f32),
        pltpu.VMEM((R, D_REC), f32),
        pltpu.VMEM((BATCH, D_REC), f32),
        pltpu.VMEM((SSM_CHUNKS, R, 2 * ST_W), f32),
        pltpu.VMEM((BATCH, SSM_GROUPS * SSM_STATE), f32),
        pltpu.VMEM((BATCH, SSM_GROUPS * SSM_STATE), f32),
        pltpu.VMEM((N_EXPERTS, LANES), f32),
        pltpu.VMEM((R, D_MODEL), bf16),
    ]
    if fused is None:
        inputs = [x]
        in_specs = [pl.BlockSpec((BATCH, TL, D_MODEL), lambda i: (0, i, 0))]
    else:
        dest_tiles, x1_prev, gates_t, yb, ln2_g, ln2_b = fused
        inputs = [dest_tiles, dest_tiles, x1_prev, gates_t, yb, ln2_g, ln2_b]
        in_specs = [pl.BlockSpec((1, 1, 2 * TC), lambda i: (i, 0, 0), memory_space=pltpu.SMEM),
                    pl.BlockSpec((1, 1, 2 * TC), lambda i: (jnp.minimum(i + 1, n_steps - 1), 0, 0),
                                 memory_space=pltpu.SMEM),
                    pl.BlockSpec((R, D_MODEL), lambda i: (i, 0)),
                    pl.BlockSpec((R, 2), lambda i: (i, 0)),
                    pl.BlockSpec(memory_space=pl.ANY),
                    _full((1, D_MODEL)), _full((1, D_MODEL))]
        scratch = scratch + [pltpu.VMEM((2, 2, TC, D_MODEL), f32), pltpu.SemaphoreType.DMA((2,))]
    in_specs = in_specs + [_full(w.shape) for w in weights]
    out_shape = (
        jax.ShapeDtypeStruct((T_TOK, D_MODEL), f32),
        jax.ShapeDtypeStruct((SUBLANES, T_TOK), i32),
        jax.ShapeDtypeStruct((SUBLANES, T_TOK), f32),
        jax.ShapeDtypeStruct((SUBLANES, T_TOK), i32),
        jax.ShapeDtypeStruct((N_EXPERTS, LANES), f32),
    )
    out_specs = (
        pl.BlockSpec((R, D_MODEL), lambda i: (i, 0)),
        pl.BlockSpec((SUBLANES, R), lambda i: (0, i)),
        pl.BlockSpec((SUBLANES, R), lambda i: (0, i)),
        pl.BlockSpec((SUBLANES, R), lambda i: (0, i)),
        pl.BlockSpec((N_EXPERTS, LANES), lambda i: (0, 0)),
    )
    return pl.pallas_call(
        functools.partial(_mixer_kernel, fused is not None), grid=(n_steps,), in_specs=in_specs,
        out_specs=out_specs, out_shape=out_shape, scratch_shapes=scratch, name="mixer",
        compiler_params=pltpu.CompilerParams(dimension_semantics=("arbitrary",), vmem_limit_bytes=VMEM_LIMIT),
    )(*inputs, *weights)


def _dispatch_kernel(tail_ref, dest_ref, x_ref, xb_hbm, zbuf, sem):
    @pl.when(pl.program_id(0) == 0)
    def _():
        zbuf[...] = jnp.zeros_like(zbuf)

        def fill(blk):
            row0 = pl.multiple_of(blk * MOE_BLK, MOE_BLK)
            return pltpu.make_async_copy(zbuf, xb_hbm.at[pl.ds(row0, MOE_BLK)], sem)

        def for_each_fill(action):
            for e in range(N_EXPERTS):
                @pl.when(tail_ref[e] >= 0)
                def _():
                    action(fill(tail_ref[e]))

            def spare(blk, c):
                action(fill(blk))
                return c
            lax.fori_loop(tail_ref[N_EXPERTS], N_BLOCKS, spare, 0)

        for_each_fill(lambda d: d.start())
        for_each_fill(lambda d: d.wait())

    def issue(j, c):
        for k in range(2):
            pltpu.make_async_copy(x_ref.at[pl.ds(j, 1)], xb_hbm.at[pl.ds(dest_ref[0, 0, k * TD + j], 1)],
                                  sem).start(priority=k)
        return c

    _unrolled_loop(TD, DMA_UNROLL, issue, 0)
    for k in range(2):
        pltpu.make_async_copy(x_ref, xb_hbm.at[pl.ds(0, TD)], sem).wait()


def _dispatch_call(tail_blocks, dest_tiles, x1):
    grid_spec = pltpu.PrefetchScalarGridSpec(
        num_scalar_prefetch=1, grid=(T_TOK // TD,),
        in_specs=[pl.BlockSpec((1, 1, 2 * TD), lambda i, tail: (i, 0, 0), memory_space=pltpu.SMEM),
                  pl.BlockSpec((TD, D_MODEL), lambda i, tail: (i, 0))],
        out_specs=pl.BlockSpec(memory_space=pl.ANY),
        scratch_shapes=[pltpu.VMEM((MOE_BLK, D_MODEL), f32), pltpu.SemaphoreType.DMA(())])
    return pl.pallas_call(
        _dispatch_kernel, grid_spec=grid_spec,
        out_shape=jax.ShapeDtypeStruct((P_ROWS, D_MODEL), f32), name="dispatch",
        compiler_params=pltpu.CompilerParams(dimension_semantics=("arbitrary",)),
    )(tail_blocks, dest_tiles, x1)


def _expert_kernel(be_ref, nb_ref, xb_ref, wg_ref, wu_ref, wd_ref, yb_ref, wg_s, wu_s, wd_s):
    i = pl.program_id(0)
    prev = be_ref[jnp.maximum(i - 1, 0)]

    @pl.when((i == 0) | (be_ref[i] != prev))
    def _():
        wg_s[...] = wg_ref[0].astype(bf16)
        wu_s[...] = wu_ref[0].astype(bf16)
        wd_s[...] = wd_ref[0].astype(bf16)

    @pl.when(i < nb_ref[0])
    def _():
        xb = xb_ref[...].astype(bf16)
        g = jnp.dot(xb, wg_s[...], preferred_element_type=f32)
        u = jnp.dot(xb, wu_s[...], preferred_element_type=f32)
        h = (g * _sigmoid(g)) * u
        yb_ref[...] = jnp.dot(h.astype(bf16), wd_s[...], preferred_element_type=f32)

    @pl.when(i >= nb_ref[0])
    def _():
        yb_ref[...] = jnp.zeros_like(yb_ref)


def _expert_call(block_e, n_used, xb, w_gate, w_up, w_down):
    def row_block(i, be, nb):
        return (jnp.minimum(i, nb[0] - 1), 0)

    grid_spec = pltpu.PrefetchScalarGridSpec(
        num_scalar_prefetch=2, grid=(N_BLOCKS,),
        in_specs=[pl.BlockSpec((MOE_BLK, D_MODEL), row_block),
                  pl.BlockSpec((1, D_MODEL, D_EXPERT), lambda i, be, nb: (be[i], 0, 0)),
                  pl.BlockSpec((1, D_MODEL, D_EXPERT), lambda i, be, nb: (be[i], 0, 0)),
                  pl.BlockSpec((1, D_EXPERT, D_MODEL), lambda i, be, nb: (be[i], 0, 0))],
        out_specs=pl.BlockSpec((MOE_BLK, D_MODEL), lambda i, be, nb: (i, 0)),
        scratch_shapes=[pltpu.VMEM((D_MODEL, D_EXPERT), bf16),
                        pltpu.VMEM((D_MODEL, D_EXPERT), bf16),
                        pltpu.VMEM((D_EXPERT, D_MODEL), bf16)])
    return pl.pallas_call(
        _expert_kernel, grid_spec=grid_spec,
        out_shape=jax.ShapeDtypeStruct((P_ROWS, D_MODEL), f32), name="experts",
        compiler_params=pltpu.CompilerParams(dimension_semantics=("arbitrary",), vmem_limit_bytes=VMEM_LIMIT),
    )(block_e, n_used, xb, w_gate, w_up, w_down)


def _combine_kernel(dest_ref, dest_next_ref, x1_ref, g_ref, yb_hbm, ln_g_ref, ln_b_ref, out_ref, ybuf, sems):
    i = pl.program_id(0)
    slot = i % 2

    @pl.when(i == 0)
    def _():
        _issue_row_gather(dest_ref, yb_hbm, ybuf, sems, 0, DMA_UNROLL)

    @pl.when(i + 1 < pl.num_programs(0))
    def _():
        _issue_row_gather(dest_next_ref, yb_hbm, ybuf, sems, 1 - slot, DMA_UNROLL)

    _wait_row_gather(yb_hbm, ybuf, sems, slot)
    res = _combine_rows(x1_ref[...], g_ref[...], ybuf[slot, 0], ybuf[slot, 1], ln_g_ref[...], ln_b_ref[...])
    out_ref[...] = jnp.swapaxes(res.reshape(TC // BATCH, BATCH, D_MODEL), 0, 1)


def _combine_call(dest_tiles, x1, gates_t, yb, ln_g, ln_b):
    n_steps = T_TOK // TC
    return pl.pallas_call(
        _combine_kernel, grid=(n_steps,),
        in_specs=[pl.BlockSpec((1, 1, 2 * TC), lambda i: (i, 0, 0), memory_space=pltpu.SMEM),
                  pl.BlockSpec((1, 1, 2 * TC), lambda i: (jnp.minimum(i + 1, n_steps - 1), 0, 0),
                               memory_space=pltpu.SMEM),
                  pl.BlockSpec((TC, D_MODEL), lambda i: (i, 0)),
                  pl.BlockSpec((TC, 2), lambda i: (i, 0)),
                  pl.BlockSpec(memory_space=pl.ANY),
                  _full((1, D_MODEL)), _full((1, D_MODEL))],
        out_specs=pl.BlockSpec((BATCH, TC // BATCH, D_MODEL), lambda i: (0, i, 0)),
        out_shape=jax.ShapeDtypeStruct((BATCH, SEQ, D_MODEL), f32),
        scratch_shapes=[pltpu.VMEM((2, 2, TC, D_MODEL), f32), pltpu.SemaphoreType.DMA((2,))],
        name="combine",
        compiler_params=pltpu.CompilerParams(dimension_semantics=("arbitrary",), vmem_limit_bytes=VMEM_LIMIT),
    )(dest_tiles, dest_tiles, x1, gates_t, yb, ln_g, ln_b)


def _slot_kernel(eid_ref, rank_ref, start_ref, dest_ref):
    sub = lax.broadcasted_iota(i32, (N_EXPERTS, TS), 0)
    starts = start_ref[...]
    dest_ref[...] = rank_ref[...]
    for k in range(2):
        e = eid_ref[k:k + 1, :]
        start = jnp.sum(jnp.where(sub == e, starts, 0), axis=0, keepdims=True)
        dest_ref[k:k + 1, :] = start + rank_ref[k:k + 1, :]


def _slot_call(eid, rank, pad_starts):
    return pl.pallas_call(
        _slot_kernel, grid=(T_TOK // TS,),
        in_specs=[pl.BlockSpec((SUBLANES, TS), lambda i: (0, i)),
                  pl.BlockSpec((SUBLANES, TS), lambda i: (0, i)),
                  _full((N_EXPERTS, 1))],
        out_specs=pl.BlockSpec((SUBLANES, TS), lambda i: (0, i)),
        out_shape=jax.ShapeDtypeStruct((SUBLANES, T_TOK), i32), name="slots",
    )(eid, rank, pad_starts.reshape(N_EXPERTS, 1))


def _tile_dest(dest, tile):
    return jnp.transpose(dest.reshape(2, T_TOK // tile, tile), (1, 0, 2)).reshape(T_TOK // tile, 1, 2 * tile)


def _block_diag(blocks):
    n, a, b = blocks.shape
    eye = jnp.eye(n, dtype=blocks.dtype)
    return jnp.einsum("nab,nm->namb", blocks, eye).reshape(n * a, n * b)


def kernel(x, w_in, conv_w, conv_b, lru_wa, lru_ba, lru_wx, lru_bx, lru_lambda, ssm_lambda_re, ssm_lambda_im, ssm_log_dt, ssm_b_re, ssm_b_im, ssm_c_re, ssm_c_im, ssm_d, w_glu, b_glu, g_rec, g_ssm, w_out, ln1_g, ln1_b, router_wg, router_bg, router_we, router_be, exp_w_gate, exp_w_up, exp_w_down, ln2_g, ln2_b):
    a_re, a_im, bb_re, bb_im = _ssm_prep(ssm_lambda_re, ssm_lambda_im, ssm_log_dt, ssm_b_re, ssm_b_im)
    n_state = SSM_GROUPS * SSM_STATE
    a_re = a_re.reshape(DEPTH, SSM_GROUPS, SSM_GROUP, SSM_STATE)[:, :, 0, :].reshape(DEPTH, 1, n_state)
    a_im = a_im.reshape(DEPTH, SSM_GROUPS, SSM_GROUP, SSM_STATE)[:, :, 0, :].reshape(DEPTH, 1, n_state)
    a_re = jnp.broadcast_to(a_re, (DEPTH, BATCH, n_state))
    a_im = jnp.broadcast_to(a_im, (DEPTH, BATCH, n_state))
    gpc = SSM_GROUPS // SSM_CHUNKS
    tri = (lax.broadcasted_iota(i32, (R_MIX, R_MIX), 0) < lax.broadcasted_iota(i32, (R_MIX, R_MIX), 1)).astype(bf16)
    half = D_REC // 2

    layers = []
    for l in range(DEPTH):
        bre = bb_re.reshape(DEPTH, SSM_CHUNKS, gpc, SSM_GROUP, SSM_STATE)[l]
        bim = bb_im.reshape(DEPTH, SSM_CHUNKS, gpc, SSM_GROUP, SSM_STATE)[l]
        bbar = jnp.stack([jnp.concatenate([_block_diag(bre[c]), _block_diag(bim[c])], axis=1)
                          for c in range(SSM_CHUNKS)]).astype(bf16)
        cre = jnp.transpose(ssm_c_re[l].reshape(SSM_CHUNKS, gpc, SSM_GROUP, SSM_STATE), (0, 1, 3, 2))
        cim = jnp.transpose(ssm_c_im[l].reshape(SSM_CHUNKS, gpc, SSM_GROUP, SSM_STATE), (0, 1, 3, 2))
        cmat = jnp.stack([jnp.concatenate([_block_diag(cre[c]), -_block_diag(cim[c])], axis=0)
                          for c in range(SSM_CHUNKS)]).astype(bf16)
        wa_d = _block_diag(lru_wa[l])
        wx_d = _block_diag(lru_wx[l])
        w_gate = jnp.stack([jnp.concatenate([wa_d[h * half:(h + 1) * half, h * half:(h + 1) * half],
                                             wx_d[h * half:(h + 1) * half, h * half:(h + 1) * half]], axis=1)
                            for h in range(2)]).astype(bf16)
        wr32 = jnp.zeros((D_MODEL, ROUTER_W), f32)
        wr32 = wr32.at[:, 0:N_EXPERT_GROUPS].set(router_wg[l])
        wr32 = wr32.at[:, E_LOGIT_OFF:E_LOGIT_OFF + N_EXPERTS].set(router_we[l])
        wr_hi = wr32.astype(bf16)
        wr_lo = (wr32 - wr_hi.astype(f32)).astype(bf16)
        rb = jnp.zeros((1, ROUTER_W), f32)
        rb = rb.at[0, 0:N_EXPERT_GROUPS].set(router_bg[l])
        rb = rb.at[0, E_LOGIT_OFF:E_LOGIT_OFF + N_EXPERTS].set(router_be[l])
        layers.append(dict(
            w_in=w_in[l].astype(bf16), conv_w=conv_w[l], conv_b=conv_b[l].reshape(1, D_REC),
            w_gate=w_gate, ba=lru_ba[l].reshape(1, D_REC), bx=lru_bx[l].reshape(1, D_REC),
            lam=lru_lambda[l].reshape(1, D_REC), abar_re=a_re[l], abar_im=a_im[l], bbar=bbar, cmat=cmat,
            dskip=ssm_d[l].reshape(1, D_SSM), w_glu=w_glu[l].astype(bf16), b_glu=b_glu[l].reshape(1, 2 * D_SSM),
            g_rec=g_rec[l].reshape(1, D_REC), g_ssm=g_ssm[l].reshape(1, D_SSM), w_out=w_out[l].astype(bf16),
            ln1_g=ln1_g[l].reshape(1, D_MODEL), ln1_b=ln1_b[l].reshape(1, D_MODEL),
            wr=jnp.concatenate([wr_hi, wr_lo], axis=1), rb=rb, tri=tri))

    wg_all = exp_w_gate.reshape(DEPTH * N_EXPERTS, D_MODEL, D_EXPERT)
    wu_all = exp_w_up.reshape(DEPTH * N_EXPERTS, D_MODEL, D_EXPERT)
    wd_all = exp_w_down.reshape(DEPTH * N_EXPERTS, D_EXPERT, D_MODEL)
    fused = None
    for l in range(DEPTH):
        x1, eid, gates, rank, cnt = _mixer_call(layers[l], x=x if l == 0 else None, fused=fused)
        counts = cnt[:, 0].astype(i32)
        n_blk = (counts + MOE_BLK - 1) // MOE_BLK
        blk_ends = jnp.cumsum(n_blk)
        blk_starts = blk_ends - n_blk
        dest = _slot_call(eid, rank, blk_starts * MOE_BLK)[0:2]
        n_used = blk_ends[-1:].astype(i32)
        blk = jnp.arange(N_BLOCKS, dtype=i32)
        block_e = jnp.minimum(jnp.sum((blk_ends[None, :] <= blk[:, None]).astype(i32), axis=1), N_EXPERTS - 1)
        tail_blocks = jnp.concatenate([jnp.where(n_blk > 0, blk_ends - 1, -1).astype(i32), n_used])
        xb = _dispatch_call(tail_blocks, _tile_dest(dest, TD), x1)
        yb = _expert_call(block_e + l * N_EXPERTS, n_used, xb, wg_all, wu_all, wd_all)
        fused = (_tile_dest(dest, TC), x1, jnp.transpose(gates[0:2]), yb,
                 ln2_g[l].reshape(1, D_MODEL), ln2_b[l].reshape(1, D_MODEL))
    return _combine_call(*fused)
```

```python
import functools
import math

import jax
import jax.numpy as jnp
from jax import lax
from jax.experimental import pallas as pl
from jax.experimental.pallas import tpu as pltpu

D_MODEL = 1024
BATCH = 8
SEQ = 4096
DEPTH = 4
D_REC = 512
D_SSM = 512
REC_HEADS = 8
REC_HEAD_DIM = 64
CONV_WIDTH = 4
LRU_C = 8.0
SSM_GROUP = 16
SSM_GROUPS = 32
SSM_STATE = 64
N_EXPERT_GROUPS = 4
EXPERTS_PER_GROUP = 8
N_EXPERTS = 32
D_EXPERT = 512
ALPHA = (2.0 * DEPTH) ** 0.25
LN_EPS = 1e-5
RMS_EPS = 1e-6

T_TOK = BATCH * SEQ

SUBLANES = 8
LANES = 128
TL = 64
R_MIX = TL * BATCH
SCAN_UNROLL = TL
SSM_CHUNKS = 4
CH_W = D_SSM // SSM_CHUNKS
ST_W = SSM_GROUPS * SSM_STATE // SSM_CHUNKS
ROUTER_W = 128
E_LOGIT_OFF = 8
MOE_BLK = 512
MOE_SUB = MOE_BLK // 2
N_BLOCKS = (T_TOK * 2) // MOE_BLK + N_EXPERTS
P_ROWS = N_BLOCKS * MOE_BLK
TD = 2048
TC = R_MIX
TS = 4096
DMA_UNROLL = 8
VMEM_LIMIT = 56 * 1024 * 1024

f32 = jnp.float32
bf16 = jnp.bfloat16
i32 = jnp.int32


def _full(shape):
    n = len(shape)
    return pl.BlockSpec(shape, lambda *_: (0,) * n)


def _of_layer(shape, l):
    n = len(shape)
    return pl.BlockSpec((None,) + tuple(shape[1:]), lambda *_: (l,) + (0,) * (n - 1))


def _sigmoid(v):
    return 1.0 / (1.0 + jnp.exp(-v))


def _gelu(v):
    return 0.5 * v * (1.0 + jnp.tanh(0.7978845608028654 * (v + 0.044715 * (v * v * v))))


def _slab(t):
    return t * BATCH if isinstance(t, int) else pl.multiple_of(t * BATCH, BATCH)


def _unrolled_loop(n_steps, unroll, body, init):
    if unroll >= n_steps:
        carry = init
        for t in range(n_steps):
            carry = body(t, carry)
        return carry

    def outer(o, carry):
        for u in range(unroll):
            carry = body(o * unroll + u, carry)
        return carry
    return lax.fori_loop(0, n_steps // unroll, outer, init)


def _ssm_prep_kernel(lre_ref, lim_ref, ldt_ref, bre_ref, bim_ref,
                     are_ref, aim_ref, bbre_ref, bbim_ref):
    lre = jnp.minimum(lre_ref[...], -1e-4)
    lim = lim_ref[...]
    dt = jnp.exp(ldt_ref[...])
    mag = jnp.exp(lre * dt)
    a_re = mag * jnp.cos(lim * dt)
    a_im = mag * jnp.sin(lim * dt)
    den = lre * lre + lim * lim
    p_re = a_re - 1.0
    p_im = a_im
    coef_re = (p_re * lre + p_im * lim) / den
    coef_im = (p_im * lre - p_re * lim) / den
    br = bre_ref[...]
    bi = bim_ref[...]
    are_ref[...] = a_re
    aim_ref[...] = a_im
    bbre_ref[...] = coef_re * br - coef_im * bi
    bbim_ref[...] = coef_re * bi + coef_im * br


def _ssm_prep(lam_re, lam_im, log_dt, b_re, b_im):
    rows = DEPTH * SSM_GROUPS * SSM_GROUP

    def rep(a):
        return jnp.broadcast_to(a[:, :, None, :], (DEPTH, SSM_GROUPS, SSM_GROUP, SSM_STATE)).reshape(rows, SSM_STATE)

    ldt = jnp.broadcast_to(log_dt[:, :, None], (DEPTH, SSM_GROUPS, SSM_STATE))
    bre_t = jnp.transpose(b_re, (0, 1, 3, 2)).reshape(rows, SSM_STATE)
    bim_t = jnp.transpose(b_im, (0, 1, 3, 2)).reshape(rows, SSM_STATE)
    shp = jax.ShapeDtypeStruct((rows, SSM_STATE), f32)
    a_re, a_im, bb_re, bb_im = pl.pallas_call(
        _ssm_prep_kernel, out_shape=(shp, shp, shp, shp), name="ssm_prep",
    )(rep(lam_re), rep(lam_im), rep(ldt), bre_t, bim_t)
    return a_re, a_im, bb_re, bb_im


def _issue_row_gather(dref, yb_hbm, ybuf, sems, s, unroll):
    def issue(j, c):
        for k in range(2):
            pltpu.make_async_copy(yb_hbm.at[pl.ds(dref[0, 0, k * TC + j], 1)],
                                  ybuf.at[s, k, pl.ds(j, 1)], sems.at[s]).start(priority=k)
        return c
    _unrolled_loop(TC, unroll, issue, 0)


def _wait_row_gather(yb_hbm, ybuf, sems, s):
    for k in range(2):
        pltpu.make_async_copy(yb_hbm.at[pl.ds(0, TC)], ybuf.at[s, k], sems.at[s]).wait()


def _combine_rows(x1, g, y0, y1, ln_g, ln_b):
    v = ALPHA * x1 + (g[:, 0:1] * y0 + g[:, 1:2] * y1)
    mu = jnp.mean(v, axis=-1, keepdims=True)
    vc = v - mu
    var = jnp.mean(vc * vc, axis=-1, keepdims=True)
    return vc * lax.rsqrt(var + LN_EPS) * ln_g + ln_b


def _mixer_kernel(fused_in, *refs):
    if fused_in:
        dest_ref, dest_next_ref, xprev_ref, g_ref, yb_hbm, ln2_g_ref, ln2_b_ref = refs[:7]
        refs = refs[7:]
        ybuf, sems = refs[-2:]
        refs = refs[:-2]
    else:
        x_ref = refs[0]
        refs = refs[1:]
    (w_in_ref, conv_w_ref, conv_b_ref, w_gate_ref, ba_ref, bx_ref, lam_ref,
     abar_re_ref, abar_im_ref, bbar_ref, cmat_ref, dskip_ref, w_glu_ref, b_glu_ref,
     g_rec_ref, g_ssm_ref, w_out_ref, ln_g_ref, ln_b_ref, wr_ref, rb_ref, tri_ref,
     x1_ref, eid_ref, gate_ref, rank_ref, cnt_ref,
     conv_buf, a_buf, b_buf, h_carry, s_buf, s_re, s_im, cnt_carry, ycat) = refs
    R = R_MIX
    halo = (CONV_WIDTH - 1) * BATCH
    step = pl.program_id(0)

    @pl.when(pl.program_id(0) == 0)
    def _():
        conv_buf[0:halo, :] = jnp.zeros((halo, D_REC), f32)
        h_carry[...] = jnp.zeros_like(h_carry)
        s_re[...] = jnp.zeros_like(s_re)
        s_im[...] = jnp.zeros_like(s_im)
        cnt_carry[...] = jnp.zeros_like(cnt_carry)

    if fused_in:
        slot = step % 2

        @pl.when(step == 0)
        def _():
            _issue_row_gather(dest_ref, yb_hbm, ybuf, sems, 0, DMA_UNROLL)

        _wait_row_gather(yb_hbm, ybuf, sems, slot)
        x = _combine_rows(xprev_ref[...], g_ref[...], ybuf[slot, 0], ybuf[slot, 1], ln2_g_ref[...], ln2_b_ref[...])
        _issue_row_gather(dest_next_ref, yb_hbm, ybuf, sems, 1 - slot, TC)
    else:
        x = jnp.swapaxes(x_ref[...], 0, 1).reshape(R, D_MODEL)
    xb = x.astype(bf16)
    gate_br = jnp.dot(xb, w_in_ref[:, 0:D_REC], preferred_element_type=f32)
    rec_br = jnp.dot(xb, w_in_ref[:, D_REC:2 * D_REC], preferred_element_type=f32)
    ssm_br = jnp.dot(xb, w_in_ref[:, 2 * D_REC:], preferred_element_type=f32)

    conv_buf[halo:halo + R, :] = rec_br
    cw = conv_w_ref[...]
    rec = conv_b_ref[...] + cw[3:4, :] * rec_br
    for k in range(CONV_WIDTH - 1):
        rec = rec + cw[k:k + 1, :] * conv_buf[k * BATCH:k * BATCH + R, :]
    conv_buf[0:halo, :] = conv_buf[R:R + halo, :]

    recb = rec.astype(bf16)
    half = D_REC // 2
    g0 = jnp.dot(recb[:, :half], w_gate_ref[0], preferred_element_type=f32)
    g1 = jnp.dot(recb[:, half:], w_gate_ref[1], preferred_element_type=f32)
    r_gate = _sigmoid(jnp.concatenate([g0[:, :half], g1[:, :half]], axis=1) + ba_ref[...])
    i_gate = _sigmoid(jnp.concatenate([g0[:, half:], g1[:, half:]], axis=1) + bx_ref[...])
    z = -lam_ref[...]
    softplus = jnp.maximum(z, 0.0) + jnp.log1p(jnp.exp(-jnp.abs(z)))
    log_a = (-LRU_C) * r_gate * softplus
    a = jnp.exp(log_a)
    mult = jnp.sqrt(-jnp.tanh(log_a) * (1.0 + a * a))
    a_buf[...] = a
    b_buf[...] = mult * (i_gate * rec)

    def lru_step(t, h):
        row = _slab(t)
        h = a_buf[pl.ds(row, BATCH), :] * h + b_buf[pl.ds(row, BATCH), :]
        b_buf[pl.ds(row, BATCH), :] = h
        return h

    h_carry[...] = _unrolled_loop(TL, SCAN_UNROLL, lru_step, h_carry[...])
    y_rec = _gelu(gate_br) * b_buf[...]
    y_rec = y_rec * lax.rsqrt(jnp.mean(y_rec * y_rec, axis=-1, keepdims=True) + RMS_EPS) * g_rec_ref[...]
    ycat[:, 0:D_REC] = y_rec.astype(bf16)

    ub = ssm_br.astype(bf16)
    ys = []
    for c in range(SSM_CHUNKS):
        sb = s_buf.at[c]
        sb[...] = jnp.dot(ub[:, c * CH_W:(c + 1) * CH_W], bbar_ref[c], preferred_element_type=f32)
        ar = abar_re_ref[:, c * ST_W:(c + 1) * ST_W]
        ai = abar_im_ref[:, c * ST_W:(c + 1) * ST_W]

        def ssm_step(t, carry, ar=ar, ai=ai, sb=sb):
            xr, xi = carry
            row = _slab(t)
            bur = sb[pl.ds(row, BATCH), 0:ST_W]
            bui = sb[pl.ds(row, BATCH), ST_W:2 * ST_W]
            nxr = ar * xr - ai * xi + bur
            nxi = ar * xi + ai * xr + bui
            sb[pl.ds(row, BATCH), 0:ST_W] = nxr
            sb[pl.ds(row, BATCH), ST_W:2 * ST_W] = nxi
            return nxr, nxi

        xr, xi = _unrolled_loop(TL, SCAN_UNROLL, ssm_step,
                                (s_re[:, c * ST_W:(c + 1) * ST_W], s_im[:, c * ST_W:(c + 1) * ST_W]))
        s_re[:, c * ST_W:(c + 1) * ST_W] = xr
        s_im[:, c * ST_W:(c + 1) * ST_W] = xi
        ys.append(jnp.dot(sb[...].astype(bf16), cmat_ref[c], preferred_element_type=f32))
    y = jnp.concatenate(ys, axis=1) + dskip_ref[...] * ssm_br
    y = _gelu(y)
    zz = jnp.dot(y.astype(bf16), w_glu_ref[...], preferred_element_type=f32) + b_glu_ref[...]
    y_ssm = zz[:, :D_SSM] * _sigmoid(zz[:, D_SSM:])
    y_ssm = y_ssm * lax.rsqrt(jnp.mean(y_ssm * y_ssm, axis=-1, keepdims=True) + RMS_EPS) * g_ssm_ref[...]
    ycat[:, D_REC:] = y_ssm.astype(bf16)

    mix = jnp.dot(ycat[...], w_out_ref[...], preferred_element_type=f32)
    v = ALPHA * x + mix
    mu = jnp.mean(v, axis=-1, keepdims=True)
    vc = v - mu
    var = jnp.mean(vc * vc, axis=-1, keepdims=True)
    x1 = vc * lax.rsqrt(var + LN_EPS) * ln_g_ref[...] + ln_b_ref[...]
    x1_ref[...] = x1

    x_hi = x1.astype(bf16)
    x_lo = (x1 - x_hi.astype(f32)).astype(bf16)
    hh = jnp.dot(x_hi, wr_ref[...], preferred_element_type=f32)
    lh = jnp.dot(x_lo, wr_ref[:, 0:ROUTER_W], preferred_element_type=f32)
    logits = hh[:, :ROUTER_W] + hh[:, ROUTER_W:] + lh + rb_ref[...]
    lt = logits.T

    gl = [lt[j:j + 1, :] for j in range(N_EXPERT_GROUPS)]
    gmax = jnp.maximum(jnp.maximum(gl[0], gl[1]), jnp.maximum(gl[2], gl[3]))
    gidx = jnp.where(gl[0] == gmax, 0, jnp.where(gl[1] == gmax, 1, jnp.where(gl[2] == gmax, 2, 3))).astype(i32)
    gsum = (jnp.exp(gl[0] - gmax) + jnp.exp(gl[1] - gmax)) + (jnp.exp(gl[2] - gmax) + jnp.exp(gl[3] - gmax))
    g_top = 1.0 / gsum
    ets = [lt[E_LOGIT_OFF + EXPERTS_PER_GROUP * g:E_LOGIT_OFF + EXPERTS_PER_GROUP * (g + 1), :]
           for g in range(N_EXPERT_GROUPS)]
    e_in = jnp.where(gidx == 0, ets[0], jnp.where(gidx == 1, ets[1], jnp.where(gidx == 2, ets[2], ets[3])))
    sub = lax.broadcasted_iota(i32, (EXPERTS_PER_GROUP, R), 0)
    m1 = jnp.max(e_in, axis=0, keepdims=True)
    i1 = jnp.min(jnp.where(e_in == m1, sub, EXPERTS_PER_GROUP), axis=0, keepdims=True)
    rest = jnp.where(sub == i1, -jnp.inf, e_in)
    m2 = jnp.max(rest, axis=0, keepdims=True)
    i2 = jnp.min(jnp.where(rest == m2, sub, EXPERTS_PER_GROUP), axis=0, keepdims=True)
    p2 = jnp.exp(m2 - m1)
    inv = g_top / (1.0 + p2)
    e1 = gidx * EXPERTS_PER_GROUP + i1
    e2 = gidx * EXPERTS_PER_GROUP + i2

    sub32 = lax.broadcasted_iota(i32, (N_EXPERTS, R), 0)
    oh1 = (sub32 == e1).astype(f32)
    oh2 = (sub32 == e2).astype(f32)
    both = oh1 + oh2
    prefix = jnp.dot(both.astype(bf16), tri_ref[...], preferred_element_type=f32)
    base = prefix + cnt_carry[:, 0:1]
    rank1 = jnp.sum(oh1 * base, axis=0, keepdims=True)
    rank2 = jnp.sum(oh2 * base, axis=0, keepdims=True)
    cnt_carry[...] = cnt_carry[...] + jnp.sum(both, axis=1, keepdims=True)
    cnt_ref[...] = cnt_carry[...]

    eid_ref[...] = jnp.zeros_like(eid_ref)
    eid_ref[0:1, :] = e1
    eid_ref[1:2, :] = e2
    rank_ref[...] = jnp.zeros_like(rank_ref)
    rank_ref[0:1, :] = rank1.astype(i32)
    rank_ref[1:2, :] = rank2.astype(i32)
    gate_ref[...] = jnp.zeros_like(gate_ref)
    gate_ref[0:1, :] = inv
    gate_ref[1:2, :] = inv * p2

    if fused_in:
        @pl.when(step == pl.num_programs(0) - 1)
        def _():
            _wait_row_gather(yb_hbm, ybuf, sems, 1 - slot)


def _mixer_call(lw, l, x=None, fused=None):
    R = R_MIX
    halo = (CONV_WIDTH - 1) * BATCH
    n_steps = SEQ // TL
    weights = [lw["w_in"], lw["conv_w"], lw["conv_b"], lw["w_gate"], lw["ba"], lw["bx"], lw["lam"],
               lw["abar_re"], lw["abar_im"], lw["bbar"], lw["cmat"], lw["dskip"], lw["w_glu"], lw["b_glu"],
               lw["g_rec"], lw["g_ssm"], lw["w_out"], lw["ln1_g"], lw["ln1_b"], lw["wr"], lw["rb"]]
    scratch = [
        pltpu.VMEM((R + halo, D_REC), f32),
        pltpu.VMEM((R, D_REC), f32),
        pltpu.VMEM((R, D_REC), f32),
        pltpu.VMEM((BATCH, D_REC), f32),
        pltpu.VMEM((SSM_CHUNKS, R, 2 * ST_W), f32),
        pltpu.VMEM((BATCH, SSM_GROUPS * SSM_STATE), f32),
        pltpu.VMEM((BATCH, SSM_GROUPS * SSM_STATE), f32),
        pltpu.VMEM((N_EXPERTS, LANES), f32),
        pltpu.VMEM((R, D_MODEL), bf16),
    ]
    if fused is None:
        inputs = [x]
        in_specs = [pl.BlockSpec((BATCH, TL, D_MODEL), lambda i: (0, i, 0))]
    else:
        dest_tiles, x1_prev, gates_t, yb = fused
        inputs = [dest_tiles, dest_tiles, x1_prev, gates_t, yb, lw["ln2_g"], lw["ln2_b"]]
        in_specs = [pl.BlockSpec((1, 1, 2 * TC), lambda i: (i, 0, 0), memory_space=pltpu.SMEM),
                    pl.BlockSpec((1, 1, 2 * TC), lambda i: (jnp.minimum(i + 1, n_steps - 1), 0, 0),
                                 memory_space=pltpu.SMEM),
                    pl.BlockSpec((R, D_MODEL), lambda i: (i, 0)),
                    pl.BlockSpec((R, 2), lambda i: (i, 0)),
                    pl.BlockSpec(memory_space=pl.ANY),
                    _of_layer(lw["ln2_g"].shape, l - 1), _of_layer(lw["ln2_b"].shape, l - 1)]
        scratch = scratch + [pltpu.VMEM((2, 2, TC, D_MODEL), f32), pltpu.SemaphoreType.DMA((2,))]
    in_specs = in_specs + [_of_layer(w.shape, l) for w in weights] + [_full(lw["tri"].shape)]
    out_shape = (
        jax.ShapeDtypeStruct((T_TOK, D_MODEL), f32),
        jax.ShapeDtypeStruct((SUBLANES, T_TOK), i32),
        jax.ShapeDtypeStruct((SUBLANES, T_TOK), f32),
        jax.ShapeDtypeStruct((SUBLANES, T_TOK), i32),
        jax.ShapeDtypeStruct((N_EXPERTS, LANES), f32),
    )
    out_specs = (
        pl.BlockSpec((R, D_MODEL), lambda i: (i, 0)),
        pl.BlockSpec((SUBLANES, R), lambda i: (0, i)),
        pl.BlockSpec((SUBLANES, R), lambda i: (0, i)),
        pl.BlockSpec((SUBLANES, R), lambda i: (0, i)),
        pl.BlockSpec((N_EXPERTS, LANES), lambda i: (0, 0)),
    )
    return pl.pallas_call(
        functools.partial(_mixer_kernel, fused is not None), grid=(n_steps,), in_specs=in_specs,
        out_specs=out_specs, out_shape=out_shape, scratch_shapes=scratch, name="mixer",
        compiler_params=pltpu.CompilerParams(dimension_semantics=("arbitrary",), vmem_limit_bytes=VMEM_LIMIT),
    )(*inputs, *weights, lw["tri"])


def _dispatch_kernel(tail_ref, dest_ref, x_ref, xb_hbm, zbuf, sem):
    @pl.when(pl.program_id(0) == 0)
    def _():
        zbuf[...] = jnp.zeros_like(zbuf)

        def fill(blk):
            row0 = pl.multiple_of(blk * MOE_BLK, MOE_BLK)
            return pltpu.make_async_copy(zbuf, xb_hbm.at[pl.ds(row0, MOE_BLK)], sem)

        def for_each_fill(action):
            for e in range(N_EXPERTS):
                @pl.when(tail_ref[e] >= 0)
                def _():
                    action(fill(tail_ref[e]))

            def spare(blk, c):
                action(fill(blk))
                return c
            lax.fori_loop(tail_ref[N_EXPERTS], N_BLOCKS, spare, 0)

        for_each_fill(lambda d: d.start())
        for_each_fill(lambda d: d.wait())

    def issue(j, c):
        for k in range(2):
            pltpu.make_async_copy(x_ref.at[pl.ds(j, 1)], xb_hbm.at[pl.ds(dest_ref[0, 0, k * TD + j], 1)],
                                  sem).start(priority=k)
        return c

    _unrolled_loop(TD, DMA_UNROLL, issue, 0)
    for k in range(2):
        pltpu.make_async_copy(x_ref, xb_hbm.at[pl.ds(0, TD)], sem).wait()


def _dispatch_call(tail_blocks, dest_tiles, x1):
    grid_spec = pltpu.PrefetchScalarGridSpec(
        num_scalar_prefetch=1, grid=(T_TOK // TD,),
        in_specs=[pl.BlockSpec((1, 1, 2 * TD), lambda i, tail: (i, 0, 0), memory_space=pltpu.SMEM),
                  pl.BlockSpec((TD, D_MODEL), lambda i, tail: (i, 0))],
        out_specs=pl.BlockSpec(memory_space=pl.ANY),
        scratch_shapes=[pltpu.VMEM((MOE_BLK, D_MODEL), f32), pltpu.SemaphoreType.DMA(())])
    return pl.pallas_call(
        _dispatch_kernel, grid_spec=grid_spec,
        out_shape=jax.ShapeDtypeStruct((P_ROWS, D_MODEL), f32), name="dispatch",
        compiler_params=pltpu.CompilerParams(dimension_semantics=("arbitrary",)),
    )(tail_blocks, dest_tiles, x1)


def _expert_kernel(be_ref, nh_ref, nb_ref, xb_ref, wg_ref, wu_ref, wd_ref, yb_ref, wg_s, wu_s, wd_s):
    del nb_ref
    i = pl.program_id(0)
    prev = be_ref[jnp.maximum(i - 1, 0)]

    @pl.when((i == 0) | (be_ref[i] != prev))
    def _():
        wg_s[...] = wg_ref[0].astype(bf16)
        wu_s[...] = wu_ref[0].astype(bf16)
        wd_s[...] = wd_ref[0].astype(bf16)

    for hb in range(MOE_BLK // MOE_SUB):
        rows = pl.ds(hb * MOE_SUB, MOE_SUB)

        @pl.when(hb < nh_ref[i])
        def _():
            xb = xb_ref[rows, :].astype(bf16)
            g = jnp.dot(xb, wg_s[...], preferred_element_type=f32)
            u = jnp.dot(xb, wu_s[...], preferred_element_type=f32)
            h = (g * _sigmoid(g)) * u
            yb_ref[rows, :] = jnp.dot(h.astype(bf16), wd_s[...], preferred_element_type=f32)

        @pl.when(hb >= nh_ref[i])
        def _():
            yb_ref[rows, :] = jnp.zeros((MOE_SUB, D_MODEL), f32)


def _expert_call(block_e, n_half, n_used, xb, w_gate, w_up, w_down):
    def row_block(i, be, nh, nb):
        return (jnp.minimum(i, nb[0] - 1), 0)

    grid_spec = pltpu.PrefetchScalarGridSpec(
        num_scalar_prefetch=3, grid=(N_BLOCKS,),
        in_specs=[pl.BlockSpec((MOE_BLK, D_MODEL), row_block),
                  pl.BlockSpec((1, D_MODEL, D_EXPERT), lambda i, be, nh, nb: (be[i], 0, 0)),
                  pl.BlockSpec((1, D_MODEL, D_EXPERT), lambda i, be, nh, nb: (be[i], 0, 0)),
                  pl.BlockSpec((1, D_EXPERT, D_MODEL), lambda i, be, nh, nb: (be[i], 0, 0))],
        out_specs=pl.BlockSpec((MOE_BLK, D_MODEL), lambda i, be, nh, nb: (i, 0)),
        scratch_shapes=[pltpu.VMEM((D_MODEL, D_EXPERT), bf16),
                        pltpu.VMEM((D_MODEL, D_EXPERT), bf16),
                        pltpu.VMEM((D_EXPERT, D_MODEL), bf16)])
    return pl.pallas_call(
        _expert_kernel, grid_spec=grid_spec,
        out_shape=jax.ShapeDtypeStruct((P_ROWS, D_MODEL), f32), name="experts",
        compiler_params=pltpu.CompilerParams(dimension_semantics=("arbitrary",), vmem_limit_bytes=VMEM_LIMIT),
    )(block_e, n_half, n_used, xb, w_gate, w_up, w_down)


def _combine_kernel(dest_ref, dest_next_ref, x1_ref, g_ref, yb_hbm, ln_g_ref, ln_b_ref, out_ref, ybuf, sems):
    i = pl.program_id(0)
    slot = i % 2

    @pl.when(i == 0)
    def _():
        _issue_row_gather(dest_ref, yb_hbm, ybuf, sems, 0, DMA_UNROLL)

    @pl.when(i + 1 < pl.num_programs(0))
    def _():
        _issue_row_gather(dest_next_ref, yb_hbm, ybuf, sems, 1 - slot, DMA_UNROLL)

    _wait_row_gather(yb_hbm, ybuf, sems, slot)
    res = _combine_rows(x1_ref[...], g_ref[...], ybuf[slot, 0], ybuf[slot, 1], ln_g_ref[...], ln_b_ref[...])
    out_ref[...] = jnp.swapaxes(res.reshape(TC // BATCH, BATCH, D_MODEL), 0, 1)


def _combine_call(dest_tiles, x1, gates_t, yb, ln_g, ln_b, l):
    n_steps = T_TOK // TC
    return pl.pallas_call(
        _combine_kernel, grid=(n_steps,),
        in_specs=[pl.BlockSpec((1, 1, 2 * TC), lambda i: (i, 0, 0), memory_space=pltpu.SMEM),
                  pl.BlockSpec((1, 1, 2 * TC), lambda i: (jnp.minimum(i + 1, n_steps - 1), 0, 0),
                               memory_space=pltpu.SMEM),
                  pl.BlockSpec((TC, D_MODEL), lambda i: (i, 0)),
                  pl.BlockSpec((TC, 2), lambda i: (i, 0)),
                  pl.BlockSpec(memory_space=pl.ANY),
                  _of_layer(ln_g.shape, l), _of_layer(ln_b.shape, l)],
        out_specs=pl.BlockSpec((BATCH, TC // BATCH, D_MODEL), lambda i: (0, i, 0)),
        out_shape=jax.ShapeDtypeStruct((BATCH, SEQ, D_MODEL), f32),
        scratch_shapes=[pltpu.VMEM((2, 2, TC, D_MODEL), f32), pltpu.SemaphoreType.DMA((2,))],
        name="combine",
        compiler_params=pltpu.CompilerParams(dimension_semantics=("arbitrary",), vmem_limit_bytes=VMEM_LIMIT),
    )(dest_tiles, dest_tiles, x1, gates_t, yb, ln_g, ln_b)


def _slot_kernel(eid_ref, rank_ref, start_ref, dest_ref):
    sub = lax.broadcasted_iota(i32, (N_EXPERTS, TS), 0)
    starts = start_ref[...]
    dest_ref[...] = rank_ref[...]
    for k in range(2):
        e = eid_ref[k:k + 1, :]
        start = jnp.sum(jnp.where(sub == e, starts, 0), axis=0, keepdims=True)
        dest_ref[k:k + 1, :] = start + rank_ref[k:k + 1, :]


def _slot_call(eid, rank, pad_starts):
    return pl.pallas_call(
        _slot_kernel, grid=(T_TOK // TS,),
        in_specs=[pl.BlockSpec((SUBLANES, TS), lambda i: (0, i)),
                  pl.BlockSpec((SUBLANES, TS), lambda i: (0, i)),
                  _full((N_EXPERTS, 1))],
        out_specs=pl.BlockSpec((SUBLANES, TS), lambda i: (0, i)),
        out_shape=jax.ShapeDtypeStruct((SUBLANES, T_TOK), i32), name="slots",
    )(eid, rank, pad_starts.reshape(N_EXPERTS, 1))


def _tile_dest(dest, tile):
    return jnp.transpose(dest.reshape(2, T_TOK // tile, tile), (1, 0, 2)).reshape(T_TOK // tile, 1, 2 * tile)


def _block_diag(blocks):
    n, a, b = blocks.shape[-3:]
    eye = jnp.eye(n, dtype=blocks.dtype)
    return jnp.einsum("...nab,nm->...namb", blocks, eye).reshape(blocks.shape[:-3] + (n * a, n * b))


def kernel(x, w_in, conv_w, conv_b, lru_wa, lru_ba, lru_wx, lru_bx, lru_lambda, ssm_lambda_re, ssm_lambda_im, ssm_log_dt, ssm_b_re, ssm_b_im, ssm_c_re, ssm_c_im, ssm_d, w_glu, b_glu, g_rec, g_ssm, w_out, ln1_g, ln1_b, router_wg, router_bg, router_we, router_be, exp_w_gate, exp_w_up, exp_w_down, ln2_g, ln2_b):
    a_re, a_im, bb_re, bb_im = _ssm_prep(ssm_lambda_re, ssm_lambda_im, ssm_log_dt, ssm_b_re, ssm_b_im)
    n_state = SSM_GROUPS * SSM_STATE
    gpc = SSM_GROUPS // SSM_CHUNKS

    def state_row(a):
        a = a.reshape(DEPTH, SSM_GROUPS, SSM_GROUP, SSM_STATE)[:, :, 0, :].reshape(DEPTH, 1, n_state)
        return jnp.broadcast_to(a, (DEPTH, BATCH, n_state))

    def chunked(b):
        return b.reshape(DEPTH, SSM_CHUNKS, gpc, SSM_GROUP, SSM_STATE)

    bbar = jnp.concatenate([_block_diag(chunked(bb_re)), _block_diag(chunked(bb_im))], axis=-1).astype(bf16)
    c_t = lambda c: jnp.swapaxes(c.reshape(DEPTH, SSM_CHUNKS, gpc, SSM_GROUP, SSM_STATE), -1, -2)
    cmat = jnp.concatenate([_block_diag(c_t(ssm_c_re)), -_block_diag(c_t(ssm_c_im))], axis=-2).astype(bf16)
    heads_per_half = REC_HEADS // 2
    halves = lambda w: _block_diag(w.reshape(DEPTH, 2, heads_per_half, REC_HEAD_DIM, REC_HEAD_DIM))
    w_gate = jnp.concatenate([halves(lru_wa), halves(lru_wx)], axis=-1).astype(bf16)
    wr32 = jnp.zeros((DEPTH, D_MODEL, ROUTER_W), f32)
    wr32 = wr32.at[:, :, 0:N_EXPERT_GROUPS].set(router_wg)
    wr32 = wr32.at[:, :, E_LOGIT_OFF:E_LOGIT_OFF + N_EXPERTS].set(router_we)
    wr_hi = wr32.astype(bf16)
    wr_lo = (wr32 - wr_hi.astype(f32)).astype(bf16)
    rb = jnp.zeros((DEPTH, 1, ROUTER_W), f32)
    rb = rb.at[:, 0, 0:N_EXPERT_GROUPS].set(router_bg)
    rb = rb.at[:, 0, E_LOGIT_OFF:E_LOGIT_OFF + N_EXPERTS].set(router_be)
    row = lambda v: v.reshape(DEPTH, 1, v.shape[-1])
    tri = (lax.broadcasted_iota(i32, (R_MIX, R_MIX), 0) < lax.broadcasted_iota(i32, (R_MIX, R_MIX), 1)).astype(bf16)
    lw = dict(
        w_in=w_in.astype(bf16), conv_w=conv_w, conv_b=row(conv_b), w_gate=w_gate, ba=row(lru_ba), bx=row(lru_bx),
        lam=row(lru_lambda), abar_re=state_row(a_re), abar_im=state_row(a_im), bbar=bbar, cmat=cmat,
        dskip=ssm_d.reshape(DEPTH, 1, D_SSM), w_glu=w_glu.astype(bf16), b_glu=row(b_glu), g_rec=row(g_rec),
        g_ssm=row(g_ssm), w_out=w_out.astype(bf16), ln1_g=row(ln1_g), ln1_b=row(ln1_b),
        wr=jnp.concatenate([wr_hi, wr_lo], axis=-1), rb=rb, tri=tri, ln2_g=row(ln2_g), ln2_b=row(ln2_b))

    wg_all = exp_w_gate.reshape(DEPTH * N_EXPERTS, D_MODEL, D_EXPERT)
    wu_all = exp_w_up.reshape(DEPTH * N_EXPERTS, D_MODEL, D_EXPERT)
    wd_all = exp_w_down.reshape(DEPTH * N_EXPERTS, D_EXPERT, D_MODEL)
    fused = None
    for l in range(DEPTH):
        x1, eid, gates, rank, cnt = _mixer_call(lw, l, x=x if l == 0 else None, fused=fused)
        counts = cnt[:, 0].astype(i32)
        n_blk = (counts + MOE_BLK - 1) // MOE_BLK
        blk_ends = jnp.cumsum(n_blk)
        blk_starts = blk_ends - n_blk
        dest = _slot_call(eid, rank, blk_starts * MOE_BLK)[0:2]
        n_used = blk_ends[-1:].astype(i32)
        blk = jnp.arange(N_BLOCKS, dtype=i32)
        block_e = jnp.minimum(jnp.sum((blk_ends[None, :] <= blk[:, None]).astype(i32), axis=1), N_EXPERTS - 1)
        tail_blocks = jnp.concatenate([jnp.where(n_blk > 0, blk_ends - 1, -1).astype(i32), n_used])
        xb = _dispatch_call(tail_blocks, _tile_dest(dest, TD), x1)
        of_expert = (block_e[:, None] == jnp.arange(N_EXPERTS, dtype=i32)[None, :]).astype(i32)
        rows_in_blk = jnp.clip(of_expert @ counts - (blk - of_expert @ blk_starts) * MOE_BLK, 0, MOE_BLK)
        n_half = jnp.where(blk < n_used[0], (rows_in_blk + MOE_SUB - 1) // MOE_SUB, 0).astype(i32)
        yb = _expert_call(block_e + l * N_EXPERTS, n_half, n_used, xb, wg_all, wu_all, wd_all)
        fused = (_tile_dest(dest, TC), x1, jnp.transpose(gates[0:2]), yb)
    return _combine_call(*fused, lw["ln2_g"], lw["ln2_b"], DEPTH - 1)
```

```python
import functools
import math

import jax
import jax.numpy as jnp
from jax import lax
from jax.experimental import pallas as pl
from jax.experimental.pallas import tpu as pltpu

D_MODEL = 1024
BATCH = 8
SEQ = 4096
DEPTH = 4
D_REC = 512
D_SSM = 512
REC_HEADS = 8
REC_HEAD_DIM = 64
CONV_WIDTH = 4
LRU_C = 8.0
SSM_GROUP = 16
SSM_GROUPS = 32
SSM_STATE = 64
N_EXPERT_GROUPS = 4
EXPERTS_PER_GROUP = 8
N_EXPERTS = 32
D_EXPERT = 512
ALPHA = (2.0 * DEPTH) ** 0.25
LN_EPS = 1e-5
RMS_EPS = 1e-6

T_TOK = BATCH * SEQ

SUBLANES = 8
LANES = 128
TL = 64
R_MIX = TL * BATCH
SCAN_UNROLL = TL
SSM_CHUNKS = 4
CH_W = D_SSM // SSM_CHUNKS
ST_W = SSM_GROUPS * SSM_STATE // SSM_CHUNKS
ROUTER_W = 128
E_LOGIT_OFF = 8
MOE_BLK = 512
MOE_SUB = MOE_BLK // 2
N_BLOCKS = (T_TOK * 2) // MOE_BLK + N_EXPERTS
P_ROWS = N_BLOCKS * MOE_BLK
TD = 2048
TC = R_MIX
TS = 4096
DMA_UNROLL = 8
VMEM_LIMIT = 56 * 1024 * 1024

f32 = jnp.float32
bf16 = jnp.bfloat16
i32 = jnp.int32


def _full(shape):
    n = len(shape)
    return pl.BlockSpec(shape, lambda *_: (0,) * n)


def _of_layer(shape, l):
    n = len(shape)
    return pl.BlockSpec((None,) + tuple(shape[1:]), lambda *_: (l,) + (0,) * (n - 1))


def _sigmoid(v):
    return 1.0 / (1.0 + jnp.exp(-v))


def _gelu(v):
    return 0.5 * v * (1.0 + jnp.tanh(0.7978845608028654 * (v + 0.044715 * (v * v * v))))


def _slab(t):
    return t * BATCH if isinstance(t, int) else pl.multiple_of(t * BATCH, BATCH)


def _unrolled_loop(n_steps, unroll, body, init):
    if unroll >= n_steps:
        carry = init
        for t in range(n_steps):
            carry = body(t, carry)
        return carry

    def outer(o, carry):
        for u in range(unroll):
            carry = body(o * unroll + u, carry)
        return carry
    return lax.fori_loop(0, n_steps // unroll, outer, init)


def _ssm_prep_kernel(lre_ref, lim_ref, ldt_ref, bre_ref, bim_ref,
                     are_ref, aim_ref, bbre_ref, bbim_ref):
    lre = jnp.minimum(lre_ref[...], -1e-4)
    lim = lim_ref[...]
    dt = jnp.exp(ldt_ref[...])
    mag = jnp.exp(lre * dt)
    a_re = mag * jnp.cos(lim * dt)
    a_im = mag * jnp.sin(lim * dt)
    den = lre * lre + lim * lim
    p_re = a_re - 1.0
    p_im = a_im
    coef_re = (p_re * lre + p_im * lim) / den
    coef_im = (p_im * lre - p_re * lim) / den
    br = bre_ref[...]
    bi = bim_ref[...]
    are_ref[...] = a_re
    aim_ref[...] = a_im
    bbre_ref[...] = coef_re * br - coef_im * bi
    bbim_ref[...] = coef_re * bi + coef_im * br


def _ssm_prep(lam_re, lam_im, log_dt, b_re, b_im):
    rows = DEPTH * SSM_GROUPS * SSM_GROUP

    def rep(a):
        return jnp.broadcast_to(a[:, :, None, :], (DEPTH, SSM_GROUPS, SSM_GROUP, SSM_STATE)).reshape(rows, SSM_STATE)

    ldt = jnp.broadcast_to(log_dt[:, :, None], (DEPTH, SSM_GROUPS, SSM_STATE))
    bre_t = jnp.transpose(b_re, (0, 1, 3, 2)).reshape(rows, SSM_STATE)
    bim_t = jnp.transpose(b_im, (0, 1, 3, 2)).reshape(rows, SSM_STATE)
    shp = jax.ShapeDtypeStruct((rows, SSM_STATE), f32)
    a_re, a_im, bb_re, bb_im = pl.pallas_call(
        _ssm_prep_kernel, out_shape=(shp, shp, shp, shp), name="ssm_prep",
    )(rep(lam_re), rep(lam_im), rep(ldt), bre_t, bim_t)
    return a_re, a_im, bb_re, bb_im


def _issue_row_gather(dref, yb_hbm, ybuf, sems, s, unroll):
    def issue(j, c):
        for k in range(2):
            pltpu.make_async_copy(yb_hbm.at[pl.ds(dref[0, 0, k * TC + j], 1)],
                                  ybuf.at[s, k, pl.ds(j, 1)], sems.at[s]).start(priority=k)
        return c
    _unrolled_loop(TC, unroll, issue, 0)


def _wait_row_gather(yb_hbm, ybuf, sems, s):
    for k in range(2):
        pltpu.make_async_copy(yb_hbm.at[pl.ds(0, TC)], ybuf.at[s, k], sems.at[s]).wait()


def _combine_rows(x1, g, y0, y1, ln_g, ln_b):
    v = ALPHA * x1 + (g[:, 0:1] * y0 + g[:, 1:2] * y1)
    mu = jnp.mean(v, axis=-1, keepdims=True)
    vc = v - mu
    var = jnp.mean(vc * vc, axis=-1, keepdims=True)
    return vc * lax.rsqrt(var + LN_EPS) * ln_g + ln_b


def _mixer_kernel(fused_in, *refs):
    if fused_in:
        dest_ref, dest_next_ref, xprev_ref, g_ref, yb_hbm, ln2_g_ref, ln2_b_ref = refs[:7]
        refs = refs[7:]
        ybuf, sems = refs[-2:]
        refs = refs[:-2]
    else:
        x_ref = refs[0]
        refs = refs[1:]
    (w_in_ref, conv_w_ref, conv_b_ref, w_gate_ref, ba_ref, bx_ref, lam_ref,
     abar_re_ref, abar_im_ref, bbar_ref, cmat_ref, dskip_ref, w_glu_ref, b_glu_ref,
     g_rec_ref, g_ssm_ref, w_out_ref, ln_g_ref, ln_b_ref, wr_ref, rb_ref, tri_ref,
     x1_ref, eid_ref, gate_ref, rank_ref, cnt_ref,
     conv_buf, a_buf, b_buf, h_carry, s_buf, s_re, s_im, cnt_carry, ycat) = refs
    R = R_MIX
    halo = (CONV_WIDTH - 1) * BATCH
    step = pl.program_id(0)

    @pl.when(pl.program_id(0) == 0)
    def _():
        conv_buf[0:halo, :] = jnp.zeros((halo, D_REC), f32)
        h_carry[...] = jnp.zeros_like(h_carry)
        s_re[...] = jnp.zeros_like(s_re)
        s_im[...] = jnp.zeros_like(s_im)
        cnt_carry[...] = jnp.zeros_like(cnt_carry)

    if fused_in:
        slot = step % 2

        @pl.when(step == 0)
        def _():
            _issue_row_gather(dest_ref, yb_hbm, ybuf, sems, 0, DMA_UNROLL)

        _wait_row_gather(yb_hbm, ybuf, sems, slot)
        x = _combine_rows(xprev_ref[...], g_ref[...], ybuf[slot, 0], ybuf[slot, 1], ln2_g_ref[...], ln2_b_ref[...])
        _issue_row_gather(dest_next_ref, yb_hbm, ybuf, sems, 1 - slot, TC)
    else:
        x = jnp.swapaxes(x_ref[...], 0, 1).reshape(R, D_MODEL)
    xb = x.astype(bf16)
    gate_br = jnp.dot(xb, w_in_ref[:, 0:D_REC], preferred_element_type=f32)
    rec_br = jnp.dot(xb, w_in_ref[:, D_REC:2 * D_REC], preferred_element_type=f32)
    ssm_br = jnp.dot(xb, w_in_ref[:, 2 * D_REC:], preferred_element_type=f32)

    conv_buf[halo:halo + R, :] = rec_br
    cw = conv_w_ref[...]
    rec = conv_b_ref[...] + cw[3:4, :] * rec_br
    for k in range(CONV_WIDTH - 1):
        rec = rec + cw[k:k + 1, :] * conv_buf[k * BATCH:k * BATCH + R, :]
    conv_buf[0:halo, :] = conv_buf[R:R + halo, :]

    recb = rec.astype(bf16)
    half = D_REC // 2
    g0 = jnp.dot(recb[:, :half], w_gate_ref[0], preferred_element_type=f32)
    g1 = jnp.dot(recb[:, half:], w_gate_ref[1], preferred_element_type=f32)
    r_gate = _sigmoid(jnp.concatenate([g0[:, :half], g1[:, :half]], axis=1) + ba_ref[...])
    i_gate = _sigmoid(jnp.concatenate([g0[:, half:], g1[:, half:]], axis=1) + bx_ref[...])
    z = -lam_ref[...]
    softplus = jnp.maximum(z, 0.0) + jnp.log1p(jnp.exp(-jnp.abs(z)))
    log_a = (-LRU_C) * r_gate * softplus
    a = jnp.exp(log_a)
    mult = jnp.sqrt(-jnp.tanh(log_a) * (1.0 + a * a))
    a_buf[...] = a
    b_buf[...] = mult * (i_gate * rec)

    def lru_step(t, h):
        row = _slab(t)
        h = a_buf[pl.ds(row, BATCH), :] * h + b_buf[pl.ds(row, BATCH), :]
        b_buf[pl.ds(row, BATCH), :] = h
        return h

    h_carry[...] = _unrolled_loop(TL, SCAN_UNROLL, lru_step, h_carry[...])
    y_rec = _gelu(gate_br) * b_buf[...]
    y_rec = y_rec * lax.rsqrt(jnp.mean(y_rec * y_rec, axis=-1, keepdims=True) + RMS_EPS) * g_rec_ref[...]
    ycat[:, 0:D_REC] = y_rec.astype(bf16)

    ub = ssm_br.astype(bf16)
    ys = []
    for c in range(SSM_CHUNKS):
        sb = s_buf.at[c]
        sb[...] = jnp.dot(ub[:, c * CH_W:(c + 1) * CH_W], bbar_ref[c], preferred_element_type=f32)
        ar = abar_re_ref[:, c * ST_W:(c + 1) * ST_W]
        ai = abar_im_ref[:, c * ST_W:(c + 1) * ST_W]

        def ssm_step(t, carry, ar=ar, ai=ai, sb=sb):
            xr, xi = carry
            row = _slab(t)
            bur = sb[pl.ds(row, BATCH), 0:ST_W]
            bui = sb[pl.ds(row, BATCH), ST_W:2 * ST_W]
            nxr = ar * xr - ai * xi + bur
            nxi = ar * xi + ai * xr + bui
            sb[pl.ds(row, BATCH), 0:ST_W] = nxr
            sb[pl.ds(row, BATCH), ST_W:2 * ST_W] = nxi
            return nxr, nxi

        xr, xi = _unrolled_loop(TL, SCAN_UNROLL, ssm_step,
                                (s_re[:, c * ST_W:(c + 1) * ST_W], s_im[:, c * ST_W:(c + 1) * ST_W]))
        s_re[:, c * ST_W:(c + 1) * ST_W] = xr
        s_im[:, c * ST_W:(c + 1) * ST_W] = xi
        ys.append(jnp.dot(sb[...].astype(bf16), cmat_ref[c], preferred_element_type=f32))
    y = jnp.concatenate(ys, axis=1) + dskip_ref[...] * ssm_br
    y = _gelu(y)
    zz = jnp.dot(y.astype(bf16), w_glu_ref[...], preferred_element_type=f32) + b_glu_ref[...]
    y_ssm = zz[:, :D_SSM] * _sigmoid(zz[:, D_SSM:])
    y_ssm = y_ssm * lax.rsqrt(jnp.mean(y_ssm * y_ssm, axis=-1, keepdims=True) + RMS_EPS) * g_ssm_ref[...]
    ycat[:, D_REC:] = y_ssm.astype(bf16)

    mix = jnp.dot(ycat[...], w_out_ref[...], preferred_element_type=f32)
    v = ALPHA * x + mix
    mu = jnp.mean(v, axis=-1, keepdims=True)
    vc = v - mu
    var = jnp.mean(vc * vc, axis=-1, keepdims=True)
    x1 = vc * lax.rsqrt(var + LN_EPS) * ln_g_ref[...] + ln_b_ref[...]
    x1_ref[...] = x1

    x_hi = x1.astype(bf16)
    x_lo = (x1 - x_hi.astype(f32)).astype(bf16)
    hh = jnp.dot(x_hi, wr_ref[...], preferred_element_type=f32)
    lh = jnp.dot(x_lo, wr_ref[:, 0:ROUTER_W], preferred_element_type=f32)
    logits = hh[:, :ROUTER_W] + hh[:, ROUTER_W:] + lh + rb_ref[...]
    lt = logits.T

    gl = [lt[j:j + 1, :] for j in range(N_EXPERT_GROUPS)]
    gmax = jnp.maximum(jnp.maximum(gl[0], gl[1]), jnp.maximum(gl[2], gl[3]))
    gidx = jnp.where(gl[0] == gmax, 0, jnp.where(gl[1] == gmax, 1, jnp.where(gl[2] == gmax, 2, 3))).astype(i32)
    gsum = (jnp.exp(gl[0] - gmax) + jnp.exp(gl[1] - gmax)) + (jnp.exp(gl[2] - gmax) + jnp.exp(gl[3] - gmax))
    g_top = 1.0 / gsum
    ets = [lt[E_LOGIT_OFF + EXPERTS_PER_GROUP * g:E_LOGIT_OFF + EXPERTS_PER_GROUP * (g + 1), :]
           for g in range(N_EXPERT_GROUPS)]
    e_in = jnp.where(gidx == 0, ets[0], jnp.where(gidx == 1, ets[1], jnp.where(gidx == 2, ets[2], ets[3])))
    sub = lax.broadcasted_iota(i32, (EXPERTS_PER_GROUP, R), 0)
    m1 = jnp.max(e_in, axis=0, keepdims=True)
    i1 = jnp.min(jnp.where(e_in == m1, sub, EXPERTS_PER_GROUP), axis=0, keepdims=True)
    rest = jnp.where(sub == i1, -jnp.inf, e_in)
    m2 = jnp.max(rest, axis=0, keepdims=True)
    i2 = jnp.min(jnp.where(rest == m2, sub, EXPERTS_PER_GROUP), axis=0, keepdims=True)
    p2 = jnp.exp(m2 - m1)
    inv = g_top / (1.0 + p2)
    e1 = gidx * EXPERTS_PER_GROUP + i1
    e2 = gidx * EXPERTS_PER_GROUP + i2

    sub32 = lax.broadcasted_iota(i32, (N_EXPERTS, R), 0)
    oh1 = (sub32 == e1).astype(f32)
    oh2 = (sub32 == e2).astype(f32)
    both = oh1 + oh2
    prefix = jnp.dot(both.astype(bf16), tri_ref[...], preferred_element_type=f32)
    base = prefix + cnt_carry[:, 0:1]
    rank1 = jnp.sum(oh1 * base, axis=0, keepdims=True)
    rank2 = jnp.sum(oh2 * base, axis=0, keepdims=True)
    cnt_carry[...] = cnt_carry[...] + jnp.sum(both, axis=1, keepdims=True)
    cnt_ref[...] = cnt_carry[...]

    eid_ref[...] = jnp.zeros_like(eid_ref)
    eid_ref[0:1, :] = e1
    eid_ref[1:2, :] = e2
    rank_ref[...] = jnp.zeros_like(rank_ref)
    rank_ref[0:1, :] = rank1.astype(i32)
    rank_ref[1:2, :] = rank2.astype(i32)
    gate_ref[...] = jnp.zeros_like(gate_ref)
    gate_ref[0:1, :] = inv
    gate_ref[1:2, :] = inv * p2

    if fused_in:
        @pl.when(step == pl.num_programs(0) - 1)
        def _():
            _wait_row_gather(yb_hbm, ybuf, sems, 1 - slot)


def _mixer_call(lw, l, x=None, fused=None):
    R = R_MIX
    halo = (CONV_WIDTH - 1) * BATCH
    n_steps = SEQ // TL
    weights = [lw["w_in"], lw["conv_w"], lw["conv_b"], lw["w_gate"], lw["ba"], lw["bx"], lw["lam"],
               lw["abar_re"], lw["abar_im"], lw["bbar"], lw["cmat"], lw["dskip"], lw["w_glu"], lw["b_glu"],
               lw["g_rec"], lw["g_ssm"], lw["w_out"], lw["ln1_g"], lw["ln1_b"], lw["wr"], lw["rb"]]
    scratch = [
        pltpu.VMEM((R + halo, D_REC), f32),
        pltpu.VMEM((R, D_REC), f32),
        pltpu.VMEM((R, D_REC), f32),
        pltpu.VMEM((BATCH, D_REC), f32),
        pltpu.VMEM((SSM_CHUNKS, R, 2 * ST_W), f32),
        pltpu.VMEM((BATCH, SSM_GROUPS * SSM_STATE), f32),
        pltpu.VMEM((BATCH, SSM_GROUPS * SSM_STATE), f32),
        pltpu.VMEM((N_EXPERTS, LANES), f32),
        pltpu.VMEM((R, D_MODEL), bf16),
    ]
    if fused is None:
        inputs = [x]
        in_specs = [pl.BlockSpec((BATCH, TL, D_MODEL), lambda i: (0, i, 0))]
    else:
        dest_tiles, x1_prev, gates_t, yb = fused
        inputs = [dest_tiles, dest_tiles, x1_prev, gates_t, yb, lw["ln2_g"], lw["ln2_b"]]
        in_specs = [pl.BlockSpec((1, 1, 2 * TC), lambda i: (i, 0, 0), memory_space=pltpu.SMEM),
                    pl.BlockSpec((1, 1, 2 * TC), lambda i: (jnp.minimum(i + 1, n_steps - 1), 0, 0),
                                 memory_space=pltpu.SMEM),
                    pl.BlockSpec((R, D_MODEL), lambda i: (i, 0)),
                    pl.BlockSpec((R, 2), lambda i: (i, 0)),
                    pl.BlockSpec(memory_space=pl.ANY),
                    _of_layer(lw["ln2_g"].shape, l - 1), _of_layer(lw["ln2_b"].shape, l - 1)]
        scratch = scratch + [pltpu.VMEM((2, 2, TC, D_MODEL), f32), pltpu.SemaphoreType.DMA((2,))]
    in_specs = in_specs + [_of_layer(w.shape, l) for w in weights] + [_full(lw["tri"].shape)]
    out_shape = (
        jax.ShapeDtypeStruct((T_TOK, D_MODEL), f32),
        jax.ShapeDtypeStruct((SUBLANES, T_TOK), i32),
        jax.ShapeDtypeStruct((SUBLANES, T_TOK), f32),
        jax.ShapeDtypeStruct((SUBLANES, T_TOK), i32),
        jax.ShapeDtypeStruct((N_EXPERTS, LANES), f32),
    )
    out_specs = (
        pl.BlockSpec((R, D_MODEL), lambda i: (i, 0)),
        pl.BlockSpec((SUBLANES, R), lambda i: (0, i)),
        pl.BlockSpec((SUBLANES, R), lambda i: (0, i)),
        pl.BlockSpec((SUBLANES, R), lambda i: (0, i)),
        pl.BlockSpec((N_EXPERTS, LANES), lambda i: (0, 0)),
    )
    return pl.pallas_call(
        functools.partial(_mixer_kernel, fused is not None), grid=(n_steps,), in_specs=in_specs,
        out_specs=out_specs, out_shape=out_shape, scratch_shapes=scratch, name="mixer",
        compiler_params=pltpu.CompilerParams(dimension_semantics=("arbitrary",), vmem_limit_bytes=VMEM_LIMIT),
    )(*inputs, *weights, lw["tri"])


def _dispatch_kernel(tail_ref, dest_ref, x_ref, xb_hbm, zbuf, sem):
    @pl.when(pl.program_id(0) == 0)
    def _():
        zbuf[...] = jnp.zeros_like(zbuf)

        def fill(blk):
            row0 = pl.multiple_of(blk * MOE_BLK, MOE_BLK)
            return pltpu.make_async_copy(zbuf, xb_hbm.at[pl.ds(row0, MOE_BLK)], sem)

        def for_each_fill(action):
            for e in range(N_EXPERTS):
                @pl.when(tail_ref[e] >= 0)
                def _():
                    action(fill(tail_ref[e]))

            def spare(blk, c):
                action(fill(blk))
                return c
            lax.fori_loop(tail_ref[N_EXPERTS], N_BLOCKS, spare, 0)

        for_each_fill(lambda d: d.start())
        for_each_fill(lambda d: d.wait())

    def issue(j, c):
        for k in range(2):
            pltpu.make_async_copy(x_ref.at[pl.ds(j, 1)], xb_hbm.at[pl.ds(dest_ref[0, 0, k * TD + j], 1)],
                                  sem).start(priority=k)
        return c

    _unrolled_loop(TD, DMA_UNROLL, issue, 0)
    for k in range(2):
        pltpu.make_async_copy(x_ref, xb_hbm.at[pl.ds(0, TD)], sem).wait()


def _dispatch_call(tail_blocks, dest_tiles, x1):
    grid_spec = pltpu.PrefetchScalarGridSpec(
        num_scalar_prefetch=1, grid=(T_TOK // TD,),
        in_specs=[pl.BlockSpec((1, 1, 2 * TD), lambda i, tail: (i, 0, 0), memory_space=pltpu.SMEM),
                  pl.BlockSpec((TD, D_MODEL), lambda i, tail: (i, 0))],
        out_specs=pl.BlockSpec(memory_space=pl.ANY),
        scratch_shapes=[pltpu.VMEM((MOE_BLK, D_MODEL), f32), pltpu.SemaphoreType.DMA(())])
    return pl.pallas_call(
        _dispatch_kernel, grid_spec=grid_spec,
        out_shape=jax.ShapeDtypeStruct((P_ROWS, D_MODEL), f32), name="dispatch",
        compiler_params=pltpu.CompilerParams(dimension_semantics=("arbitrary",)),
    )(tail_blocks, dest_tiles, x1)


def _expert_kernel(be_ref, nh_ref, nb_ref, xb_ref, wg_ref, wu_ref, wd_ref, yb_ref, wg_s, wu_s, wd_s):
    del nb_ref
    i = pl.program_id(0)
    prev = be_ref[jnp.maximum(i - 1, 0)]

    @pl.when((i == 0) | (be_ref[i] != prev))
    def _():
        wg_s[...] = wg_ref[0].astype(bf16)
        wu_s[...] = wu_ref[0].astype(bf16)
        wd_s[...] = wd_ref[0].astype(bf16)

    def swiglu(rows):
        xb = xb_ref[rows, :].astype(bf16)
        g = jnp.dot(xb, wg_s[...], preferred_element_type=f32)
        u = jnp.dot(xb, wu_s[...], preferred_element_type=f32)
        h = (g * _sigmoid(g)) * u
        yb_ref[rows, :] = jnp.dot(h.astype(bf16), wd_s[...], preferred_element_type=f32)

    first, second = pl.ds(0, MOE_SUB), pl.ds(MOE_SUB, MOE_SUB)

    @pl.when(nh_ref[i] == 2)
    def _():
        swiglu(pl.ds(0, MOE_BLK))

    @pl.when(nh_ref[i] == 1)
    def _():
        swiglu(first)
        yb_ref[second, :] = jnp.zeros((MOE_SUB, D_MODEL), f32)

    @pl.when(nh_ref[i] == 0)
    def _():
        yb_ref[...] = jnp.zeros_like(yb_ref)


def _expert_call(block_e, n_half, n_used, xb, w_gate, w_up, w_down):
    def row_block(i, be, nh, nb):
        return (jnp.minimum(i, nb[0] - 1), 0)

    grid_spec = pltpu.PrefetchScalarGridSpec(
        num_scalar_prefetch=3, grid=(N_BLOCKS,),
        in_specs=[pl.BlockSpec((MOE_BLK, D_MODEL), row_block),
                  pl.BlockSpec((1, D_MODEL, D_EXPERT), lambda i, be, nh, nb: (be[i], 0, 0)),
                  pl.BlockSpec((1, D_MODEL, D_EXPERT), lambda i, be, nh, nb: (be[i], 0, 0)),
                  pl.BlockSpec((1, D_EXPERT, D_MODEL), lambda i, be, nh, nb: (be[i], 0, 0))],
        out_specs=pl.BlockSpec((MOE_BLK, D_MODEL), lambda i, be, nh, nb: (i, 0)),
        scratch_shapes=[pltpu.VMEM((D_MODEL, D_EXPERT), bf16),
                        pltpu.VMEM((D_MODEL, D_EXPERT), bf16),
                        pltpu.VMEM((D_EXPERT, D_MODEL), bf16)])
    return pl.pallas_call(
        _expert_kernel, grid_spec=grid_spec,
        out_shape=jax.ShapeDtypeStruct((P_ROWS, D_MODEL), f32), name="experts",
        compiler_params=pltpu.CompilerParams(dimension_semantics=("arbitrary",), vmem_limit_bytes=VMEM_LIMIT),
    )(block_e, n_half, n_used, xb, w_gate, w_up, w_down)


def _combine_kernel(dest_ref, dest_next_ref, x1_ref, g_ref, yb_hbm, ln_g_ref, ln_b_ref, out_ref, ybuf, sems):
    i = pl.program_id(0)
    slot = i % 2

    @pl.when(i == 0)
    def _():
        _issue_row_gather(dest_ref, yb_hbm, ybuf, sems, 0, DMA_UNROLL)

    @pl.when(i + 1 < pl.num_programs(0))
    def _():
        _issue_row_gather(dest_next_ref, yb_hbm, ybuf, sems, 1 - slot, DMA_UNROLL)

    _wait_row_gather(yb_hbm, ybuf, sems, slot)
    res = _combine_rows(x1_ref[...], g_ref[...], ybuf[slot, 0], ybuf[slot, 1], ln_g_ref[...], ln_b_ref[...])
    out_ref[...] = jnp.swapaxes(res.reshape(TC // BATCH, BATCH, D_MODEL), 0, 1)


def _combine_call(dest_tiles, x1, gates_t, yb, ln_g, ln_b, l):
    n_steps = T_TOK // TC
    return pl.pallas_call(
        _combine_kernel, grid=(n_steps,),
        in_specs=[pl.BlockSpec((1, 1, 2 * TC), lambda i: (i, 0, 0), memory_space=pltpu.SMEM),
                  pl.BlockSpec((1, 1, 2 * TC), lambda i: (jnp.minimum(i + 1, n_steps - 1), 0, 0),
                               memory_space=pltpu.SMEM),
                  pl.BlockSpec((TC, D_MODEL), lambda i: (i, 0)),
                  pl.BlockSpec((TC, 2), lambda i: (i, 0)),
                  pl.BlockSpec(memory_space=pl.ANY),
                  _of_layer(ln_g.shape, l), _of_layer(ln_b.shape, l)],
        out_specs=pl.BlockSpec((BATCH, TC // BATCH, D_MODEL), lambda i: (0, i, 0)),
        out_shape=jax.ShapeDtypeStruct((BATCH, SEQ, D_MODEL), f32),
        scratch_shapes=[pltpu.VMEM((2, 2, TC, D_MODEL), f32), pltpu.SemaphoreType.DMA((2,))],
        name="combine",
        compiler_params=pltpu.CompilerParams(dimension_semantics=("arbitrary",), vmem_limit_bytes=VMEM_LIMIT),
    )(dest_tiles, dest_tiles, x1, gates_t, yb, ln_g, ln_b)


def _slot_kernel(eid_ref, rank_ref, start_ref, dest_ref):
    sub = lax.broadcasted_iota(i32, (N_EXPERTS, TS), 0)
    starts = start_ref[...]
    dest_ref[...] = rank_ref[...]
    for k in range(2):
        e = eid_ref[k:k + 1, :]
        start = jnp.sum(jnp.where(sub == e, starts, 0), axis=0, keepdims=True)
        dest_ref[k:k + 1, :] = start + rank_ref[k:k + 1, :]


def _slot_call(eid, rank, pad_starts):
    return pl.pallas_call(
        _slot_kernel, grid=(T_TOK // TS,),
        in_specs=[pl.BlockSpec((SUBLANES, TS), lambda i: (0, i)),
                  pl.BlockSpec((SUBLANES, TS), lambda i: (0, i)),
                  _full((N_EXPERTS, 1))],
        out_specs=pl.BlockSpec((SUBLANES, TS), lambda i: (0, i)),
        out_shape=jax.ShapeDtypeStruct((SUBLANES, T_TOK), i32), name="slots",
    )(eid, rank, pad_starts.reshape(N_EXPERTS, 1))


def _tile_dest(dest, tile):
    return jnp.transpose(dest.reshape(2, T_TOK // tile, tile), (1, 0, 2)).reshape(T_TOK // tile, 1, 2 * tile)


def _block_diag(blocks):
    n, a, b = blocks.shape[-3:]
    eye = jnp.eye(n, dtype=blocks.dtype)
    return jnp.einsum("...nab,nm->...namb", blocks, eye).reshape(blocks.shape[:-3] + (n * a, n * b))


def kernel(x, w_in, conv_w, conv_b, lru_wa, lru_ba, lru_wx, lru_bx, lru_lambda, ssm_lambda_re, ssm_lambda_im, ssm_log_dt, ssm_b_re, ssm_b_im, ssm_c_re, ssm_c_im, ssm_d, w_glu, b_glu, g_rec, g_ssm, w_out, ln1_g, ln1_b, router_wg, router_bg, router_we, router_be, exp_w_gate, exp_w_up, exp_w_down, ln2_g, ln2_b):
    a_re, a_im, bb_re, bb_im = _ssm_prep(ssm_lambda_re, ssm_lambda_im, ssm_log_dt, ssm_b_re, ssm_b_im)
    n_state = SSM_GROUPS * SSM_STATE
    gpc = SSM_GROUPS // SSM_CHUNKS

    def state_row(a):
        a = a.reshape(DEPTH, SSM_GROUPS, SSM_GROUP, SSM_STATE)[:, :, 0, :].reshape(DEPTH, 1, n_state)
        return jnp.broadcast_to(a, (DEPTH, BATCH, n_state))

    def chunked(b):
        return b.reshape(DEPTH, SSM_CHUNKS, gpc, SSM_GROUP, SSM_STATE)

    bbar = jnp.concatenate([_block_diag(chunked(bb_re)), _block_diag(chunked(bb_im))], axis=-1).astype(bf16)
    c_t = lambda c: jnp.swapaxes(c.reshape(DEPTH, SSM_CHUNKS, gpc, SSM_GROUP, SSM_STATE), -1, -2)
    cmat = jnp.concatenate([_block_diag(c_t(ssm_c_re)), -_block_diag(c_t(ssm_c_im))], axis=-2).astype(bf16)
    heads_per_half = REC_HEADS // 2
    halves = lambda w: _block_diag(w.reshape(DEPTH, 2, heads_per_half, REC_HEAD_DIM, REC_HEAD_DIM))
    w_gate = jnp.concatenate([halves(lru_wa), halves(lru_wx)], axis=-1).astype(bf16)
    wr32 = jnp.zeros((DEPTH, D_MODEL, ROUTER_W), f32)
    wr32 = wr32.at[:, :, 0:N_EXPERT_GROUPS].set(router_wg)
    wr32 = wr32.at[:, :, E_LOGIT_OFF:E_LOGIT_OFF + N_EXPERTS].set(router_we)
    wr_hi = wr32.astype(bf16)
    wr_lo = (wr32 - wr_hi.astype(f32)).astype(bf16)
    rb = jnp.zeros((DEPTH, 1, ROUTER_W), f32)
    rb = rb.at[:, 0, 0:N_EXPERT_GROUPS].set(router_bg)
    rb = rb.at[:, 0, E_LOGIT_OFF:E_LOGIT_OFF + N_EXPERTS].set(router_be)
    row = lambda v: v.reshape(DEPTH, 1, v.shape[-1])
    tri = (lax.broadcasted_iota(i32, (R_MIX, R_MIX), 0) < lax.broadcasted_iota(i32, (R_MIX, R_MIX), 1)).astype(bf16)
    lw = dict(
        w_in=w_in.astype(bf16), conv_w=conv_w, conv_b=row(conv_b), w_gate=w_gate, ba=row(lru_ba), bx=row(lru_bx),
        lam=row(lru_lambda), abar_re=state_row(a_re), abar_im=state_row(a_im), bbar=bbar, cmat=cmat,
        dskip=ssm_d.reshape(DEPTH, 1, D_SSM), w_glu=w_glu.astype(bf16), b_glu=row(b_glu), g_rec=row(g_rec),
        g_ssm=row(g_ssm), w_out=w_out.astype(bf16), ln1_g=row(ln1_g), ln1_b=row(ln1_b),
        wr=jnp.concatenate([wr_hi, wr_lo], axis=-1), rb=rb, tri=tri, ln2_g=row(ln2_g), ln2_b=row(ln2_b))

    wg_all = exp_w_gate.reshape(DEPTH * N_EXPERTS, D_MODEL, D_EXPERT)
    wu_all = exp_w_up.reshape(DEPTH * N_EXPERTS, D_MODEL, D_EXPERT)
    wd_all = exp_w_down.reshape(DEPTH * N_EXPERTS, D_EXPERT, D_MODEL)
    fused = None
    for l in range(DEPTH):
        x1, eid, gates, rank, cnt = _mixer_call(lw, l, x=x if l == 0 else None, fused=fused)
        counts = cnt[:, 0].astype(i32)
        n_blk = (counts + MOE_BLK - 1) // MOE_BLK
        blk_ends = jnp.cumsum(n_blk)
        blk_starts = blk_ends - n_blk
        dest = _slot_call(eid, rank, blk_starts * MOE_BLK)[0:2]
        n_used = blk_ends[-1:].astype(i32)
        blk = jnp.arange(N_BLOCKS, dtype=i32)
        block_e = jnp.minimum(jnp.sum((blk_ends[None, :] <= blk[:, None]).astype(i32), axis=1), N_EXPERTS - 1)
        tail_blocks = jnp.concatenate([jnp.where(n_blk > 0, blk_ends - 1, -1).astype(i32), n_used])
        xb = _dispatch_call(tail_blocks, _tile_dest(dest, TD), x1)
        of_expert = (block_e[:, None] == jnp.arange(N_EXPERTS, dtype=i32)[None, :]).astype(i32)
        rows_in_blk = jnp.clip(of_expert @ counts - (blk - of_expert @ blk_starts) * MOE_BLK, 0, MOE_BLK)
        n_half = jnp.where(blk < n_used[0], (rows_in_blk + MOE_SUB - 1) // MOE_SUB, 0).astype(i32)
        yb = _expert_call(block_e + l * N_EXPERTS, n_half, n_used, xb, wg_all, wu_all, wd_all)
        fused = (_tile_dest(dest, TC), x1, jnp.transpose(gates[0:2]), yb)
    return _combine_call(*fused, lw["ln2_g"], lw["ln2_b"], DEPTH - 1)
```

```python
import functools
import math

import jax
import jax.numpy as jnp
from jax import lax
from jax.experimental import pallas as pl
from jax.experimental.pallas import tpu as pltpu

D_MODEL = 1024
BATCH = 8
SEQ = 4096
DEPTH = 4
D_REC = 512
D_SSM = 512
REC_HEADS = 8
REC_HEAD_DIM = 64
CONV_WIDTH = 4
LRU_C = 8.0
SSM_GROUP = 16
SSM_GROUPS = 32
SSM_STATE = 64
N_EXPERT_GROUPS = 4
EXPERTS_PER_GROUP = 8
N_EXPERTS = 32
D_EXPERT = 512
ALPHA = (2.0 * DEPTH) ** 0.25
LN_EPS = 1e-5
RMS_EPS = 1e-6

T_TOK = BATCH * SEQ

SUBLANES = 8
LANES = 128
TL = 64
R_MIX = TL * BATCH
SCAN_UNROLL = TL
SSM_CHUNKS = 4
CH_W = D_SSM // SSM_CHUNKS
ST_W = SSM_GROUPS * SSM_STATE // SSM_CHUNKS
ROUTER_W = 128
E_LOGIT_OFF = 8
MOE_BLK = 512
MOE_SUB = MOE_BLK // 2
N_BLOCKS = (T_TOK * 2) // MOE_BLK + N_EXPERTS
P_ROWS = N_BLOCKS * MOE_BLK
TD = 4096
TC = R_MIX
TS = 4096
DMA_UNROLL = 8
VMEM_LIMIT = 56 * 1024 * 1024

f32 = jnp.float32
bf16 = jnp.bfloat16
i32 = jnp.int32


def _full(shape):
    n = len(shape)
    return pl.BlockSpec(shape, lambda *_: (0,) * n)


def _of_layer(shape, l):
    n = len(shape)
    return pl.BlockSpec((None,) + tuple(shape[1:]), lambda *_: (l,) + (0,) * (n - 1))


def _sigmoid(v):
    return 1.0 / (1.0 + jnp.exp(-v))


def _gelu(v):
    return 0.5 * v * (1.0 + jnp.tanh(0.7978845608028654 * (v + 0.044715 * (v * v * v))))


def _slab(t):
    return t * BATCH if isinstance(t, int) else pl.multiple_of(t * BATCH, BATCH)


def _unrolled_loop(n_steps, unroll, body, init):
    if unroll >= n_steps:
        carry = init
        for t in range(n_steps):
            carry = body(t, carry)
        return carry

    def outer(o, carry):
        for u in range(unroll):
            carry = body(o * unroll + u, carry)
        return carry
    return lax.fori_loop(0, n_steps // unroll, outer, init)


def _ssm_prep_kernel(lre_ref, lim_ref, ldt_ref, bre_ref, bim_ref,
                     are_ref, aim_ref, bbre_ref, bbim_ref):
    lre = jnp.minimum(lre_ref[...], -1e-4)
    lim = lim_ref[...]
    dt = jnp.exp(ldt_ref[...])
    mag = jnp.exp(lre * dt)
    a_re = mag * jnp.cos(lim * dt)
    a_im = mag * jnp.sin(lim * dt)
    den = lre * lre + lim * lim
    p_re = a_re - 1.0
    p_im = a_im
    coef_re = (p_re * lre + p_im * lim) / den
    coef_im = (p_im * lre - p_re * lim) / den
    br = bre_ref[...]
    bi = bim_ref[...]
    are_ref[...] = a_re
    aim_ref[...] = a_im
    bbre_ref[...] = coef_re * br - coef_im * bi
    bbim_ref[...] = coef_re * bi + coef_im * br


def _ssm_prep(lam_re, lam_im, log_dt, b_re, b_im):
    rows = DEPTH * SSM_GROUPS * SSM_GROUP

    def rep(a):
        return jnp.broadcast_to(a[:, :, None, :], (DEPTH, SSM_GROUPS, SSM_GROUP, SSM_STATE)).reshape(rows, SSM_STATE)

    ldt = jnp.broadcast_to(log_dt[:, :, None], (DEPTH, SSM_GROUPS, SSM_STATE))
    bre_t = jnp.transpose(b_re, (0, 1, 3, 2)).reshape(rows, SSM_STATE)
    bim_t = jnp.transpose(b_im, (0, 1, 3, 2)).reshape(rows, SSM_STATE)
    shp = jax.ShapeDtypeStruct((rows, SSM_STATE), f32)
    a_re, a_im, bb_re, bb_im = pl.pallas_call(
        _ssm_prep_kernel, out_shape=(shp, shp, shp, shp), name="ssm_prep",
    )(rep(lam_re), rep(lam_im), rep(ldt), bre_t, bim_t)
    return a_re, a_im, bb_re, bb_im


def _issue_row_gather(dref, yb_hbm, ybuf, sems, s, unroll):
    def issue(j, c):
        for k in range(2):
            pltpu.make_async_copy(yb_hbm.at[pl.ds(dref[0, 0, k * TC + j], 1)],
                                  ybuf.at[s, k, pl.ds(j, 1)], sems.at[s]).start(priority=k)
        return c
    _unrolled_loop(TC, unroll, issue, 0)


def _wait_row_gather(yb_hbm, ybuf, sems, s):
    for k in range(2):
        pltpu.make_async_copy(yb_hbm.at[pl.ds(0, TC)], ybuf.at[s, k], sems.at[s]).wait()


def _combine_rows(x1, g, y0, y1, ln_g, ln_b):
    v = ALPHA * x1 + (g[:, 0:1] * y0 + g[:, 1:2] * y1)
    mu = jnp.mean(v, axis=-1, keepdims=True)
    vc = v - mu
    var = jnp.mean(vc * vc, axis=-1, keepdims=True)
    return vc * lax.rsqrt(var + LN_EPS) * ln_g + ln_b


def _mixer_kernel(fused_in, *refs):
    if fused_in:
        dest_ref, dest_next_ref, xprev_ref, g_ref, yb_hbm, ln2_g_ref, ln2_b_ref = refs[:7]
        refs = refs[7:]
        ybuf, sems = refs[-2:]
        refs = refs[:-2]
    else:
        x_ref = refs[0]
        refs = refs[1:]
    (w_in_ref, conv_w_ref, conv_b_ref, w_gate_ref, ba_ref, bx_ref, lam_ref,
     abar_re_ref, abar_im_ref, bbar_ref, cmat_ref, dskip_ref, w_glu_ref, b_glu_ref,
     g_rec_ref, g_ssm_ref, w_out_ref, ln_g_ref, ln_b_ref, wr_ref, rb_ref, tri_ref,
     x1_ref, eid_ref, gate_ref, rank_ref, cnt_ref,
     conv_buf, a_buf, b_buf, h_carry, s_buf, s_re, s_im, cnt_carry, ycat) = refs
    R = R_MIX
    halo = (CONV_WIDTH - 1) * BATCH
    step = pl.program_id(0)

    @pl.when(pl.program_id(0) == 0)
    def _():
        conv_buf[0:halo, :] = jnp.zeros((halo, D_REC), f32)
        h_carry[...] = jnp.zeros_like(h_carry)
        s_re[...] = jnp.zeros_like(s_re)
        s_im[...] = jnp.zeros_like(s_im)
        cnt_carry[...] = jnp.zeros_like(cnt_carry)

    if fused_in:
        slot = step % 2

        @pl.when(step == 0)
        def _():
            _issue_row_gather(dest_ref, yb_hbm, ybuf, sems, 0, DMA_UNROLL)

        _wait_row_gather(yb_hbm, ybuf, sems, slot)
        x = _combine_rows(xprev_ref[...], g_ref[...], ybuf[slot, 0], ybuf[slot, 1], ln2_g_ref[...], ln2_b_ref[...])
        _issue_row_gather(dest_next_ref, yb_hbm, ybuf, sems, 1 - slot, TC)
    else:
        x = jnp.swapaxes(x_ref[...], 0, 1).reshape(R, D_MODEL)
    xb = x.astype(bf16)
    gate_br = jnp.dot(xb, w_in_ref[:, 0:D_REC], preferred_element_type=f32)
    rec_br = jnp.dot(xb, w_in_ref[:, D_REC:2 * D_REC], preferred_element_type=f32)
    ssm_br = jnp.dot(xb, w_in_ref[:, 2 * D_REC:], preferred_element_type=f32)

    conv_buf[halo:halo + R, :] = rec_br
    cw = conv_w_ref[...]
    rec = conv_b_ref[...] + cw[3:4, :] * rec_br
    for k in range(CONV_WIDTH - 1):
        rec = rec + cw[k:k + 1, :] * conv_buf[k * BATCH:k * BATCH + R, :]
    conv_buf[0:halo, :] = conv_buf[R:R + halo, :]

    recb = rec.astype(bf16)
    half = D_REC // 2
    g0 = jnp.dot(recb[:, :half], w_gate_ref[0], preferred_element_type=f32)
    g1 = jnp.dot(recb[:, half:], w_gate_ref[1], preferred_element_type=f32)
    r_gate = _sigmoid(jnp.concatenate([g0[:, :half], g1[:, :half]], axis=1) + ba_ref[...])
    i_gate = _sigmoid(jnp.concatenate([g0[:, half:], g1[:, half:]], axis=1) + bx_ref[...])
    z = -lam_ref[...]
    softplus = jnp.maximum(z, 0.0) + jnp.log1p(jnp.exp(-jnp.abs(z)))
    log_a = (-LRU_C) * r_gate * softplus
    a = jnp.exp(log_a)
    mult = jnp.sqrt(-jnp.tanh(log_a) * (1.0 + a * a))
    a_buf[...] = a
    b_buf[...] = mult * (i_gate * rec)

    def lru_step(t, h):
        row = _slab(t)
        h = a_buf[pl.ds(row, BATCH), :] * h + b_buf[pl.ds(row, BATCH), :]
        b_buf[pl.ds(row, BATCH), :] = h
        return h

    h_carry[...] = _unrolled_loop(TL, SCAN_UNROLL, lru_step, h_carry[...])
    y_rec = _gelu(gate_br) * b_buf[...]
    y_rec = y_rec * lax.rsqrt(jnp.mean(y_rec * y_rec, axis=-1, keepdims=True) + RMS_EPS) * g_rec_ref[...]
    ycat[:, 0:D_REC] = y_rec.astype(bf16)

    ub = ssm_br.astype(bf16)
    ys = []
    for c in range(SSM_CHUNKS):
        sb = s_buf.at[c]
        sb[...] = jnp.dot(ub[:, c * CH_W:(c + 1) * CH_W], bbar_ref[c], preferred_element_type=f32)
        ar = abar_re_ref[:, c * ST_W:(c + 1) * ST_W]
        ai = abar_im_ref[:, c * ST_W:(c + 1) * ST_W]

        def ssm_step(t, carry, ar=ar, ai=ai, sb=sb):
            xr, xi = carry
            row = _slab(t)
            bur = sb[pl.ds(row, BATCH), 0:ST_W]
            bui = sb[pl.ds(row, BATCH), ST_W:2 * ST_W]
            nxr = ar * xr - ai * xi + bur
            nxi = ar * xi + ai * xr + bui
            sb[pl.ds(row, BATCH), 0:ST_W] = nxr
            sb[pl.ds(row, BATCH), ST_W:2 * ST_W] = nxi
            return nxr, nxi

        xr, xi = _unrolled_loop(TL, SCAN_UNROLL, ssm_step,
                                (s_re[:, c * ST_W:(c + 1) * ST_W], s_im[:, c * ST_W:(c + 1) * ST_W]))
        s_re[:, c * ST_W:(c + 1) * ST_W] = xr
        s_im[:, c * ST_W:(c + 1) * ST_W] = xi
        ys.append(jnp.dot(sb[...].astype(bf16), cmat_ref[c], preferred_element_type=f32))
    y = jnp.concatenate(ys, axis=1) + dskip_ref[...] * ssm_br
    y = _gelu(y)
    zz = jnp.dot(y.astype(bf16), w_glu_ref[...], preferred_element_type=f32) + b_glu_ref[...]
    y_ssm = zz[:, :D_SSM] * _sigmoid(zz[:, D_SSM:])
    y_ssm = y_ssm * lax.rsqrt(jnp.mean(y_ssm * y_ssm, axis=-1, keepdims=True) + RMS_EPS) * g_ssm_ref[...]
    ycat[:, D_REC:] = y_ssm.astype(bf16)

    mix = jnp.dot(ycat[...], w_out_ref[...], preferred_element_type=f32)
    v = ALPHA * x + mix
    mu = jnp.mean(v, axis=-1, keepdims=True)
    vc = v - mu
    var = jnp.mean(vc * vc, axis=-1, keepdims=True)
    x1 = vc * lax.rsqrt(var + LN_EPS) * ln_g_ref[...] + ln_b_ref[...]
    x1_ref[...] = x1

    x_hi = x1.astype(bf16)
    x_lo = (x1 - x_hi.astype(f32)).astype(bf16)
    hh = jnp.dot(x_hi, wr_ref[...], preferred_element_type=f32)
    lh = jnp.dot(x_lo, wr_ref[:, 0:ROUTER_W], preferred_element_type=f32)
    logits = hh[:, :ROUTER_W] + hh[:, ROUTER_W:] + lh + rb_ref[...]
    lt = logits.T

    gl = [lt[j:j + 1, :] for j in range(N_EXPERT_GROUPS)]
    gmax = jnp.maximum(jnp.maximum(gl[0], gl[1]), jnp.maximum(gl[2], gl[3]))
    gidx = jnp.where(gl[0] == gmax, 0, jnp.where(gl[1] == gmax, 1, jnp.where(gl[2] == gmax, 2, 3))).astype(i32)
    gsum = (jnp.exp(gl[0] - gmax) + jnp.exp(gl[1] - gmax)) + (jnp.exp(gl[2] - gmax) + jnp.exp(gl[3] - gmax))
    g_top = 1.0 / gsum
    ets = [lt[E_LOGIT_OFF + EXPERTS_PER_GROUP * g:E_LOGIT_OFF + EXPERTS_PER_GROUP * (g + 1), :]
           for g in range(N_EXPERT_GROUPS)]
    e_in = jnp.where(gidx == 0, ets[0], jnp.where(gidx == 1, ets[1], jnp.where(gidx == 2, ets[2], ets[3])))
    sub = lax.broadcasted_iota(i32, (EXPERTS_PER_GROUP, R), 0)
    m1 = jnp.max(e_in, axis=0, keepdims=True)
    i1 = jnp.min(jnp.where(e_in == m1, sub, EXPERTS_PER_GROUP), axis=0, keepdims=True)
    rest = jnp.where(sub == i1, -jnp.inf, e_in)
    m2 = jnp.max(rest, axis=0, keepdims=True)
    i2 = jnp.min(jnp.where(rest == m2, sub, EXPERTS_PER_GROUP), axis=0, keepdims=True)
    p2 = jnp.exp(m2 - m1)
    inv = g_top / (1.0 + p2)
    e1 = gidx * EXPERTS_PER_GROUP + i1
    e2 = gidx * EXPERTS_PER_GROUP + i2

    sub32 = lax.broadcasted_iota(i32, (N_EXPERTS, R), 0)
    oh1 = (sub32 == e1).astype(f32)
    oh2 = (sub32 == e2).astype(f32)
    both = oh1 + oh2
    prefix = jnp.dot(both.astype(bf16), tri_ref[...], preferred_element_type=f32)
    base = prefix + cnt_carry[:, 0:1]
    rank1 = jnp.sum(oh1 * base, axis=0, keepdims=True)
    rank2 = jnp.sum(oh2 * base, axis=0, keepdims=True)
    cnt_carry[...] = cnt_carry[...] + jnp.sum(both, axis=1, keepdims=True)
    cnt_ref[...] = cnt_carry[...]

    eid_ref[...] = jnp.zeros_like(eid_ref)
    eid_ref[0:1, :] = e1
    eid_ref[1:2, :] = e2
    rank_ref[...] = jnp.zeros_like(rank_ref)
    rank_ref[0:1, :] = rank1.astype(i32)
    rank_ref[1:2, :] = rank2.astype(i32)
    gate_ref[...] = jnp.zeros_like(gate_ref)
    gate_ref[0:1, :] = inv
    gate_ref[1:2, :] = inv * p2

    if fused_in:
        @pl.when(step == pl.num_programs(0) - 1)
        def _():
            _wait_row_gather(yb_hbm, ybuf, sems, 1 - slot)


def _mixer_call(lw, l, x=None, fused=None):
    R = R_MIX
    halo = (CONV_WIDTH - 1) * BATCH
    n_steps = SEQ // TL
    weights = [lw["w_in"], lw["conv_w"], lw["conv_b"], lw["w_gate"], lw["ba"], lw["bx"], lw["lam"],
               lw["abar_re"], lw["abar_im"], lw["bbar"], lw["cmat"], lw["dskip"], lw["w_glu"], lw["b_glu"],
               lw["g_rec"], lw["g_ssm"], lw["w_out"], lw["ln1_g"], lw["ln1_b"], lw["wr"], lw["rb"]]
    scratch = [
        pltpu.VMEM((R + halo, D_REC), f32),
        pltpu.VMEM((R, D_REC), f32),
        pltpu.VMEM((R, D_REC), f32),
        pltpu.VMEM((BATCH, D_REC), f32),
        pltpu.VMEM((SSM_CHUNKS, R, 2 * ST_W), f32),
        pltpu.VMEM((BATCH, SSM_GROUPS * SSM_STATE), f32),
        pltpu.VMEM((BATCH, SSM_GROUPS * SSM_STATE), f32),
        pltpu.VMEM((N_EXPERTS, LANES), f32),
        pltpu.VMEM((R, D_MODEL), bf16),
    ]
    if fused is None:
        inputs = [x]
        in_specs = [pl.BlockSpec((BATCH, TL, D_MODEL), lambda i: (0, i, 0))]
    else:
        dest_tiles, x1_prev, gates_t, yb = fused
        inputs = [dest_tiles, dest_tiles, x1_prev, gates_t, yb, lw["ln2_g"], lw["ln2_b"]]
        in_specs = [pl.BlockSpec((1, 1, 2 * TC), lambda i: (i, 0, 0), memory_space=pltpu.SMEM),
                    pl.BlockSpec((1, 1, 2 * TC), lambda i: (jnp.minimum(i + 1, n_steps - 1), 0, 0),
                                 memory_space=pltpu.SMEM),
                    pl.BlockSpec((R, D_MODEL), lambda i: (i, 0)),
                    pl.BlockSpec((R, 2), lambda i: (i, 0)),
                    pl.BlockSpec(memory_space=pl.ANY),
                    _of_layer(lw["ln2_g"].shape, l - 1), _of_layer(lw["ln2_b"].shape, l - 1)]
        scratch = scratch + [pltpu.VMEM((2, 2, TC, D_MODEL), f32), pltpu.SemaphoreType.DMA((2,))]
    in_specs = in_specs + [_of_layer(w.shape, l) for w in weights] + [_full(lw["tri"].shape)]
    out_shape = (
        jax.ShapeDtypeStruct((T_TOK, D_MODEL), f32),
        jax.ShapeDtypeStruct((SUBLANES, T_TOK), i32),
        jax.ShapeDtypeStruct((SUBLANES, T_TOK), f32),
        jax.ShapeDtypeStruct((SUBLANES, T_TOK), i32),
        jax.ShapeDtypeStruct((N_EXPERTS, LANES), f32),
    )
    out_specs = (
        pl.BlockSpec((R, D_MODEL), lambda i: (i, 0)),
        pl.BlockSpec((SUBLANES, R), lambda i: (0, i)),
        pl.BlockSpec((SUBLANES, R), lambda i: (0, i)),
        pl.BlockSpec((SUBLANES, R), lambda i: (0, i)),
        pl.BlockSpec((N_EXPERTS, LANES), lambda i: (0, 0)),
    )
    return pl.pallas_call(
        functools.partial(_mixer_kernel, fused is not None), grid=(n_steps,), in_specs=in_specs,
        out_specs=out_specs, out_shape=out_shape, scratch_shapes=scratch, name="mixer",
        compiler_params=pltpu.CompilerParams(dimension_semantics=("arbitrary",), vmem_limit_bytes=VMEM_LIMIT),
    )(*inputs, *weights, lw["tri"])


def _dispatch_kernel(tail_ref, dest_ref, x_ref, xb_hbm, zbuf, sem):
    @pl.when(pl.program_id(0) == 0)
    def _():
        zbuf[...] = jnp.zeros_like(zbuf)

        def fill(blk):
            row0 = pl.multiple_of(blk * MOE_BLK, MOE_BLK)
            return pltpu.make_async_copy(zbuf, xb_hbm.at[pl.ds(row0, MOE_BLK)], sem)

        def for_each_fill(action):
            for e in range(N_EXPERTS):
                @pl.when(tail_ref[e] >= 0)
                def _():
                    action(fill(tail_ref[e]))

            def spare(blk, c):
                action(fill(blk))
                return c
            lax.fori_loop(tail_ref[N_EXPERTS], N_BLOCKS, spare, 0)

        for_each_fill(lambda d: d.start())
        for_each_fill(lambda d: d.wait())

    def issue(j, c):
        for k in range(2):
            pltpu.make_async_copy(x_ref.at[pl.ds(j, 1)], xb_hbm.at[pl.ds(dest_ref[0, 0, k * TD + j], 1)],
                                  sem).start(priority=k)
        return c

    _unrolled_loop(TD, DMA_UNROLL, issue, 0)
    for k in range(2):
        pltpu.make_async_copy(x_ref, xb_hbm.at[pl.ds(0, TD)], sem).wait()


def _dispatch_call(tail_blocks, dest_tiles, x1):
    grid_spec = pltpu.PrefetchScalarGridSpec(
        num_scalar_prefetch=1, grid=(T_TOK // TD,),
        in_specs=[pl.BlockSpec((1, 1, 2 * TD), lambda i, tail: (i, 0, 0), memory_space=pltpu.SMEM),
                  pl.BlockSpec((TD, D_MODEL), lambda i, tail: (i, 0))],
        out_specs=pl.BlockSpec(memory_space=pl.ANY),
        scratch_shapes=[pltpu.VMEM((MOE_BLK, D_MODEL), f32), pltpu.SemaphoreType.DMA(())])
    return pl.pallas_call(
        _dispatch_kernel, grid_spec=grid_spec,
        out_shape=jax.ShapeDtypeStruct((P_ROWS, D_MODEL), f32), name="dispatch",
        compiler_params=pltpu.CompilerParams(dimension_semantics=("arbitrary",), vmem_limit_bytes=VMEM_LIMIT),
    )(tail_blocks, dest_tiles, x1)


def _expert_kernel(be_ref, nh_ref, nb_ref, xb_ref, wg_ref, wu_ref, wd_ref, yb_ref, wg_s, wu_s, wd_s):
    del nb_ref
    i = pl.program_id(0)
    prev = be_ref[jnp.maximum(i - 1, 0)]

    @pl.when((i == 0) | (be_ref[i] != prev))
    def _():
        wg_s[...] = wg_ref[0].astype(bf16)
        wu_s[...] = wu_ref[0].astype(bf16)
        wd_s[...] = wd_ref[0].astype(bf16)

    def swiglu(rows):
        xb = xb_ref[rows, :].astype(bf16)
        g = jnp.dot(xb, wg_s[...], preferred_element_type=f32)
        u = jnp.dot(xb, wu_s[...], preferred_element_type=f32)
        h = (g * _sigmoid(g)) * u
        yb_ref[rows, :] = jnp.dot(h.astype(bf16), wd_s[...], preferred_element_type=f32)

    first, second = pl.ds(0, MOE_SUB), pl.ds(MOE_SUB, MOE_SUB)

    @pl.when(nh_ref[i] == 2)
    def _():
        swiglu(pl.ds(0, MOE_BLK))

    @pl.when(nh_ref[i] == 1)
    def _():
        swiglu(first)
        yb_ref[second, :] = jnp.zeros((MOE_SUB, D_MODEL), f32)

    @pl.when(nh_ref[i] == 0)
    def _():
        yb_ref[...] = jnp.zeros_like(yb_ref)


def _expert_call(block_e, n_half, n_used, xb, w_gate, w_up, w_down):
    def row_block(i, be, nh, nb):
        return (jnp.minimum(i, nb[0] - 1), 0)

    grid_spec = pltpu.PrefetchScalarGridSpec(
        num_scalar_prefetch=3, grid=(N_BLOCKS,),
        in_specs=[pl.BlockSpec((MOE_BLK, D_MODEL), row_block),
                  pl.BlockSpec((1, D_MODEL, D_EXPERT), lambda i, be, nh, nb: (be[i], 0, 0)),
                  pl.BlockSpec((1, D_MODEL, D_EXPERT), lambda i, be, nh, nb: (be[i], 0, 0)),
                  pl.BlockSpec((1, D_EXPERT, D_MODEL), lambda i, be, nh, nb: (be[i], 0, 0))],
        out_specs=pl.BlockSpec((MOE_BLK, D_MODEL), lambda i, be, nh, nb: (i, 0)),
        scratch_shapes=[pltpu.VMEM((D_MODEL, D_EXPERT), bf16),
                        pltpu.VMEM((D_MODEL, D_EXPERT), bf16),
                        pltpu.VMEM((D_EXPERT, D_MODEL), bf16)])
    return pl.pallas_call(
        _expert_kernel, grid_spec=grid_spec,
        out_shape=jax.ShapeDtypeStruct((P_ROWS, D_MODEL), f32), name="experts",
        compiler_params=pltpu.CompilerParams(dimension_semantics=("arbitrary",), vmem_limit_bytes=VMEM_LIMIT),
    )(block_e, n_half, n_used, xb, w_gate, w_up, w_down)


def _combine_kernel(dest_ref, dest_next_ref, x1_ref, g_ref, yb_hbm, ln_g_ref, ln_b_ref, out_ref, ybuf, sems):
    i = pl.program_id(0)
    slot = i % 2

    @pl.when(i == 0)
    def _():
        _issue_row_gather(dest_ref, yb_hbm, ybuf, sems, 0, DMA_UNROLL)

    @pl.when(i + 1 < pl.num_programs(0))
    def _():
        _issue_row_gather(dest_next_ref, yb_hbm, ybuf, sems, 1 - slot, DMA_UNROLL)

    _wait_row_gather(yb_hbm, ybuf, sems, slot)
    res = _combine_rows(x1_ref[...], g_ref[...], ybuf[slot, 0], ybuf[slot, 1], ln_g_ref[...], ln_b_ref[...])
    out_ref[...] = jnp.swapaxes(res.reshape(TC // BATCH, BATCH, D_MODEL), 0, 1)


def _combine_call(dest_tiles, x1, gates_t, yb, ln_g, ln_b, l):
    n_steps = T_TOK // TC
    return pl.pallas_call(
        _combine_kernel, grid=(n_steps,),
        in_specs=[pl.BlockSpec((1, 1, 2 * TC), lambda i: (i, 0, 0), memory_space=pltpu.SMEM),
                  pl.BlockSpec((1, 1, 2 * TC), lambda i: (jnp.minimum(i + 1, n_steps - 1), 0, 0),
                               memory_space=pltpu.SMEM),
                  pl.BlockSpec((TC, D_MODEL), lambda i: (i, 0)),
                  pl.BlockSpec((TC, 2), lambda i: (i, 0)),
                  pl.BlockSpec(memory_space=pl.ANY),
                  _of_layer(ln_g.shape, l), _of_layer(ln_b.shape, l)],
        out_specs=pl.BlockSpec((BATCH, TC // BATCH, D_MODEL), lambda i: (0, i, 0)),
        out_shape=jax.ShapeDtypeStruct((BATCH, SEQ, D_MODEL), f32),
        scratch_shapes=[pltpu.VMEM((2, 2, TC, D_MODEL), f32), pltpu.SemaphoreType.DMA((2,))],
        name="combine",
        compiler_params=pltpu.CompilerParams(dimension_semantics=("arbitrary",), vmem_limit_bytes=VMEM_LIMIT),
    )(dest_tiles, dest_tiles, x1, gates_t, yb, ln_g, ln_b)


def _slot_kernel(eid_ref, rank_ref, start_ref, dest_ref):
    sub = lax.broadcasted_iota(i32, (N_EXPERTS, TS), 0)
    starts = start_ref[...]
    dest_ref[...] = rank_ref[...]
    for k in range(2):
        e = eid_ref[k:k + 1, :]
        start = jnp.sum(jnp.where(sub == e, starts, 0), axis=0, keepdims=True)
        dest_ref[k:k + 1, :] = start + rank_ref[k:k + 1, :]


def _slot_call(eid, rank, pad_starts):
    return pl.pallas_call(
        _slot_kernel, grid=(T_TOK // TS,),
        in_specs=[pl.BlockSpec((SUBLANES, TS), lambda i: (0, i)),
                  pl.BlockSpec((SUBLANES, TS), lambda i: (0, i)),
                  _full((N_EXPERTS, 1))],
        out_specs=pl.BlockSpec((SUBLANES, TS), lambda i: (0, i)),
        out_shape=jax.ShapeDtypeStruct((SUBLANES, T_TOK), i32), name="slots",
    )(eid, rank, pad_starts.reshape(N_EXPERTS, 1))


def _tile_dest(dest, tile):
    return jnp.transpose(dest.reshape(2, T_TOK // tile, tile), (1, 0, 2)).reshape(T_TOK // tile, 1, 2 * tile)


def _block_diag(blocks):
    n, a, b = blocks.shape[-3:]
    eye = jnp.eye(n, dtype=blocks.dtype)
    return jnp.einsum("...nab,nm->...namb", blocks, eye).reshape(blocks.shape[:-3] + (n * a, n * b))


def kernel(x, w_in, conv_w, conv_b, lru_wa, lru_ba, lru_wx, lru_bx, lru_lambda, ssm_lambda_re, ssm_lambda_im, ssm_log_dt, ssm_b_re, ssm_b_im, ssm_c_re, ssm_c_im, ssm_d, w_glu, b_glu, g_rec, g_ssm, w_out, ln1_g, ln1_b, router_wg, router_bg, router_we, router_be, exp_w_gate, exp_w_up, exp_w_down, ln2_g, ln2_b):
    a_re, a_im, bb_re, bb_im = _ssm_prep(ssm_lambda_re, ssm_lambda_im, ssm_log_dt, ssm_b_re, ssm_b_im)
    n_state = SSM_GROUPS * SSM_STATE
    gpc = SSM_GROUPS // SSM_CHUNKS

    def state_row(a):
        a = a.reshape(DEPTH, SSM_GROUPS, SSM_GROUP, SSM_STATE)[:, :, 0, :].reshape(DEPTH, 1, n_state)
        return jnp.broadcast_to(a, (DEPTH, BATCH, n_state))

    def chunked(b):
        return b.reshape(DEPTH, SSM_CHUNKS, gpc, SSM_GROUP, SSM_STATE)

    bbar = jnp.concatenate([_block_diag(chunked(bb_re)), _block_diag(chunked(bb_im))], axis=-1).astype(bf16)
    c_t = lambda c: jnp.swapaxes(c.reshape(DEPTH, SSM_CHUNKS, gpc, SSM_GROUP, SSM_STATE), -1, -2)
    cmat = jnp.concatenate([_block_diag(c_t(ssm_c_re)), -_block_diag(c_t(ssm_c_im))], axis=-2).astype(bf16)
    heads_per_half = REC_HEADS // 2
    halves = lambda w: _block_diag(w.reshape(DEPTH, 2, heads_per_half, REC_HEAD_DIM, REC_HEAD_DIM))
    w_gate = jnp.concatenate([halves(lru_wa), halves(lru_wx)], axis=-1).astype(bf16)
    wr32 = jnp.zeros((DEPTH, D_MODEL, ROUTER_W), f32)
    wr32 = wr32.at[:, :, 0:N_EXPERT_GROUPS].set(router_wg)
    wr32 = wr32.at[:, :, E_LOGIT_OFF:E_LOGIT_OFF + N_EXPERTS].set(router_we)
    wr_hi = wr32.astype(bf16)
    wr_lo = (wr32 - wr_hi.astype(f32)).astype(bf16)
    rb = jnp.zeros((DEPTH, 1, ROUTER_W), f32)
    rb = rb.at[:, 0, 0:N_EXPERT_GROUPS].set(router_bg)
    rb = rb.at[:, 0, E_LOGIT_OFF:E_LOGIT_OFF + N_EXPERTS].set(router_be)
    row = lambda v: v.reshape(DEPTH, 1, v.shape[-1])
    tri = (lax.broadcasted_iota(i32, (R_MIX, R_MIX), 0) < lax.broadcasted_iota(i32, (R_MIX, R_MIX), 1)).astype(bf16)
    lw = dict(
        w_in=w_in.astype(bf16), conv_w=conv_w, conv_b=row(conv_b), w_gate=w_gate, ba=row(lru_ba), bx=row(lru_bx),
        lam=row(lru_lambda), abar_re=state_row(a_re), abar_im=state_row(a_im), bbar=bbar, cmat=cmat,
        dskip=ssm_d.reshape(DEPTH, 1, D_SSM), w_glu=w_glu.astype(bf16), b_glu=row(b_glu), g_rec=row(g_rec),
        g_ssm=row(g_ssm), w_out=w_out.astype(bf16), ln1_g=row(ln1_g), ln1_b=row(ln1_b),
        wr=jnp.concatenate([wr_hi, wr_lo], axis=-1), rb=rb, tri=tri, ln2_g=row(ln2_g), ln2_b=row(ln2_b))

    wg_all = exp_w_gate.reshape(DEPTH * N_EXPERTS, D_MODEL, D_EXPERT)
    wu_all = exp_w_up.reshape(DEPTH * N_EXPERTS, D_MODEL, D_EXPERT)
    wd_all = exp_w_down.reshape(DEPTH * N_EXPERTS, D_EXPERT, D_MODEL)
    fused = None
    for l in range(DEPTH):
        x1, eid, gates, rank, cnt = _mixer_call(lw, l, x=x if l == 0 else None, fused=fused)
        counts = cnt[:, 0].astype(i32)
        n_blk = (counts + MOE_BLK - 1) // MOE_BLK
        blk_ends = jnp.cumsum(n_blk)
        blk_starts = blk_ends - n_blk
        dest = _slot_call(eid, rank, blk_starts * MOE_BLK)[0:2]
        n_used = blk_ends[-1:].astype(i32)
        blk = jnp.arange(N_BLOCKS, dtype=i32)
        block_e = jnp.minimum(jnp.sum((blk_ends[None, :] <= blk[:, None]).astype(i32), axis=1), N_EXPERTS - 1)
        tail_blocks = jnp.concatenate([jnp.where(n_blk > 0, blk_ends - 1, -1).astype(i32), n_used])
        xb = _dispatch_call(tail_blocks, _tile_dest(dest, TD), x1)
        of_expert = (block_e[:, None] == jnp.arange(N_EXPERTS, dtype=i32)[None, :]).astype(i32)
        rows_in_blk = jnp.clip(of_expert @ counts - (blk - of_expert @ blk_starts) * MOE_BLK, 0, MOE_BLK)
        n_half = jnp.where(blk < n_used[0], (rows_in_blk + MOE_SUB - 1) // MOE_SUB, 0).astype(i32)
        yb = _expert_call(block_e + l * N_EXPERTS, n_half, n_used, xb, wg_all, wu_all, wd_all)
        fused = (_tile_dest(dest, TC), x1, jnp.transpose(gates[0:2]), yb)
    return _combine_call(*fused, lw["ln2_g"], lw["ln2_b"], DEPTH - 1)
```
